```python
import math, functools
import jax, jax.numpy as jnp
from jax import lax
import numpy as np

D_MODEL = 1024
BATCH = 4
SEQ = 8192
DEPTH = 2
DEC_BATCH = 32
DEC_SEQ = 1
PAST_LEN = 16384
PAGE_SIZE = 128

SB_HEADS = D_MODEL // 128
SB_HEAD_DIM = 64
SB_WIDTH = SB_HEADS * SB_HEAD_DIM
DA_HEADS = D_MODEL // 256
DA_HEAD_DIM = 64
DA_QK_WIDTH = DA_HEADS * 2 * DA_HEAD_DIM
DA_V_WIDTH = DA_HEADS * 2 * DA_HEAD_DIM
MIX_WIDTH = SB_WIDTH + DA_V_WIDTH
IN_SPLITS = (SB_WIDTH, 2 * SB_WIDTH, 3 * SB_WIDTH, 3 * SB_WIDTH + DA_QK_WIDTH, 3 * SB_WIDTH + 2 * DA_QK_WIDTH)
IN_WIDTH = 3 * SB_WIDTH + 2 * DA_QK_WIDTH + DA_V_WIDTH
MEM_TOKENS = 256
MEM_HEADS = 4
MEM_HEAD_DIM = 128
MEM_WIDTH = MEM_HEADS * MEM_HEAD_DIM
N_GROUPS = 4
EXPERTS_PER_GROUP = 8
N_EXPERTS = N_GROUPS * EXPERTS_PER_GROUP
TOP_K_IN_GROUP = 2
D_EXPERT = 512
Q_BLOCK = 128
MOE_BLOCK = 128
RMS_EPS = 1e-6

kernel_name = 'hybrid_stickbreak_diffattn_hmoe_decode_step'

F32 = jnp.float32


def rms_norm(x, g):
    xf = x.astype(F32)
    y = xf * lax.rsqrt(jnp.mean(xf * xf, axis=-1, keepdims=True) + RMS_EPS)
    return (y * g.astype(F32)).astype(x.dtype)


def alibi_slopes(n):
    return jnp.exp2(-8.0 * jnp.arange(1, n + 1, dtype=F32) / n)


def diff_lambda(lq1, lk1, lq2, lk2, lam_init):
    f = lambda a, b: jnp.exp(jnp.sum(a.astype(F32) * b.astype(F32)))
    return f(lq1, lk1) - f(lq2, lk2) + lam_init


def mixer_inputs(h, w_in, g_qn, g_kn):
    lead = h.shape[:-1]
    sb_q, sb_k, sb_v, da_q, da_k, da_v = jnp.split(h @ w_in, IN_SPLITS, axis=-1)
    sb = lambda t: t.reshape(*lead, SB_HEADS, SB_HEAD_DIM)

    def qk_norm(t, g):
        t = rms_norm(t.reshape(*lead, DA_HEADS, 2, DA_HEAD_DIM), g)
        return t.reshape(*lead, DA_HEADS, 2 * DA_HEAD_DIM)

    return (sb(sb_q), sb(sb_k), sb(sb_v), qk_norm(da_q, g_qn), qk_norm(da_k, g_kn),
            da_v.reshape(*lead, DA_HEADS, 2 * DA_HEAD_DIM))


def sb_attend(q, k, v, q_pos, k_pos):
    z = jnp.einsum('bqhd,bkhd->bhqk', q, k, preferred_element_type=F32) * (SB_HEAD_DIM ** -0.5)
    mask = k_pos[None, :] < q_pos[:, None]
    log_beta = jax.nn.log_sigmoid(z)
    log_1m = jnp.where(mask, jax.nn.log_sigmoid(-z), 0.0)
    after = lax.cumsum(log_1m, axis=3, reverse=True) - log_1m
    w = jnp.where(mask, jnp.exp(log_beta + after), 0.0)
    return jnp.einsum('bhqk,bkhd->bqhd', w.astype(v.dtype), v)


def diff_attend(q, k, v, q_pos, k_pos, lam):
    q1, q2 = q[..., :DA_HEAD_DIM], q[..., DA_HEAD_DIM:]
    k1, k2 = k[..., :DA_HEAD_DIM], k[..., DA_HEAD_DIM:]
    dist = (q_pos[:, None] - k_pos[None, :]).astype(F32)
    bias = -alibi_slopes(DA_HEADS)[:, None, None] * dist
    valid = dist >= 0.0

    def probs(qa, ka):
        s = jnp.einsum('bqhd,bkhd->bhqk', qa, ka, preferred_element_type=F32) * (DA_HEAD_DIM ** -0.5) + bias
        return jax.nn.softmax(jnp.where(valid, s, -jnp.inf), axis=-1)

    a = probs(q1, k1) - lam * probs(q2, k2)
    return jnp.einsum('bhqk,bkhd->bqhd', a.astype(v.dtype), v)


def sweep_query_blocks(attend, q, k, v):
    b, s = q.shape[0], q.shape[1]
    nb = s // Q_BLOCK
    qb = jnp.swapaxes(q.reshape(b, nb, Q_BLOCK, *q.shape[2:]), 0, 1)
    k_pos = jnp.arange(s)

    def one(args):
        q_blk, i = args
        return attend(q_blk, k, v, i * Q_BLOCK + jnp.arange(Q_BLOCK), k_pos)

    o = lax.map(one, (qb, jnp.arange(nb)))
    return jnp.swapaxes(o, 0, 1).reshape(b, s, *o.shape[3:])


def with_past(pool, layer, page_table, new):
    past = pool[layer, page_table]
    past = past.reshape(past.shape[0], -1, *past.shape[3:])
    return jnp.concatenate([past.astype(new.dtype), new], axis=1)


def merge_heads(sb_o, da_o, g_sb, g_da, lam_init, w_o):
    lead = sb_o.shape[:-2]
    sb_o = rms_norm(sb_o, g_sb).reshape(*lead, SB_WIDTH)
    da_o = (rms_norm(da_o, g_da) * (1.0 - lam_init)).reshape(*lead, DA_V_WIDTH)
    return jnp.concatenate([sb_o, da_o], axis=-1) @ w_o


def memory_kv(mem, g_src, w_mk, w_mv, g_kn):
    m = rms_norm(mem, g_src)
    lead = m.shape[:-1]
    k = rms_norm((m @ w_mk).reshape(*lead, MEM_HEADS, MEM_HEAD_DIM), g_kn)
    v = (m @ w_mv).reshape(*lead, MEM_HEADS, MEM_HEAD_DIM)
    return k, v


def memory_attend(h, mk, mv, w_mq, g_qn, w_mo):
    lead = h.shape[:-1]
    q = rms_norm((h @ w_mq).reshape(*lead, MEM_HEADS, MEM_HEAD_DIM), g_qn)
    s = jnp.einsum('bthd,bmhd->bhtm', q, mk, preferred_element_type=F32) * (MEM_HEAD_DIM ** -0.5)
    p = jax.nn.softmax(s, axis=-1).astype(mv.dtype)
    o = jnp.einsum('bhtm,bmhd->bthd', p, mv).reshape(*lead, MEM_WIDTH)
    return o @ w_mo


def experts_apply(t, combine, w_gate, w_up, w_down):
    g = jnp.einsum('td,edf->tef', t, w_gate)
    u = jnp.einsum('td,edf->tef', t, w_up)
    a = jax.nn.silu(g) * u * combine[..., None]
    return jnp.einsum('tef,efd->td', a, w_down)


def hmoe(h, w_rg, b_rg, w_re, b_re, w_gate, w_up, w_down):
    lead = h.shape[:-1]
    t = h.reshape(-1, D_MODEL)
    n = t.shape[0]
    g_logits = jnp.einsum('td,dg->tg', t, w_rg, preferred_element_type=F32) + b_rg.astype(F32)
    g_sel = jnp.argmax(g_logits, axis=-1)
    p_g = jnp.take_along_axis(jax.nn.softmax(g_logits, axis=-1), g_sel[:, None], axis=-1)
    e_logits = (jnp.einsum('td,de->te', t, w_re, preferred_element_type=F32) + b_re.astype(F32)).reshape(n, N_GROUPS, EXPERTS_PER_GROUP)
    e_in = jnp.take_along_axis(e_logits, g_sel[:, None, None], axis=1)[:, 0]
    top_v, top_i = lax.top_k(e_in, TOP_K_IN_GROUP)
    top_w = jax.nn.softmax(top_v, axis=-1) * p_g
    idx = g_sel[:, None] * EXPERTS_PER_GROUP + top_i
    combine = jnp.einsum('tk,tke->te', top_w, jax.nn.one_hot(idx, N_EXPERTS, dtype=F32)).astype(t.dtype)
    if n % MOE_BLOCK == 0 and n > MOE_BLOCK:
        out = lax.map(lambda a: experts_apply(a[0], a[1], w_gate, w_up, w_down),
                      (t.reshape(-1, MOE_BLOCK, D_MODEL), combine.reshape(-1, MOE_BLOCK, N_EXPERTS)))
    else:
        out = experts_apply(t, combine, w_gate, w_up, w_down)
    return out.reshape(*lead, D_MODEL)


def setup_inputs(seed: int = 0) -> dict:
    key = jax.random.key(seed)
    ks = iter(jax.random.split(key, 48))
    d = D_MODEL

    def nrm(shape, scale=1.0):
        return jax.random.normal(next(ks), shape, F32) * scale

    def gain(shape):
        return 1.0 + nrm(shape, 0.02)

    n_pages = PAST_LEN // PAGE_SIZE
    n_pool = (5 * DEC_BATCH * n_pages) // 4
    page_table = jax.random.permutation(next(ks), n_pool)[:DEC_BATCH * n_pages].reshape(DEC_BATCH, n_pages).astype(jnp.int32)
    return {
        'x_prompt': nrm((BATCH, SEQ, d)),
        'x_sample': nrm((DEC_BATCH, DEC_SEQ, d)),
        'mem_prompt': nrm((BATCH, MEM_TOKENS, d)),
        'cache_sb_k': nrm((DEPTH, n_pool, PAGE_SIZE, SB_HEADS, SB_HEAD_DIM)),
        'cache_sb_v': nrm((DEPTH, n_pool, PAGE_SIZE, SB_HEADS, SB_HEAD_DIM)),
        'cache_da_k': nrm((DEPTH, n_pool, PAGE_SIZE, DA_HEADS, 2 * DA_HEAD_DIM)),
        'cache_da_v': nrm((DEPTH, n_pool, PAGE_SIZE, DA_HEADS, 2 * DA_HEAD_DIM)),
        'cache_mem_k': nrm((DEPTH, DEC_BATCH, MEM_TOKENS, MEM_HEADS, MEM_HEAD_DIM)),
        'cache_mem_v': nrm((DEPTH, DEC_BATCH, MEM_TOKENS, MEM_HEADS, MEM_HEAD_DIM)),
        'page_table': page_table,
        'g_attn': gain((DEPTH, d)),
        'w_in': nrm((DEPTH, d, IN_WIDTH), d ** -0.5),
        'g_da_qn': gain((DEPTH, DA_HEAD_DIM)),
        'g_da_kn': gain((DEPTH, DA_HEAD_DIM)),
        'lam_q1': nrm((DEPTH, DA_HEAD_DIM), 0.1),
        'lam_k1': nrm((DEPTH, DA_HEAD_DIM), 0.1),
        'lam_q2': nrm((DEPTH, DA_HEAD_DIM), 0.1),
        'lam_k2': nrm((DEPTH, DA_HEAD_DIM), 0.1),
        'g_sb_out': gain((DEPTH, SB_HEAD_DIM)),
        'g_da_out': gain((DEPTH, 2 * DA_HEAD_DIM)),
        'w_o': nrm((DEPTH, MIX_WIDTH, d), MIX_WIDTH ** -0.5),
        'g_mem_x': gain((DEPTH, d)),
        'g_mem_src': gain((DEPTH, d)),
        'w_mq': nrm((DEPTH, d, MEM_WIDTH), d ** -0.5),
        'w_mk': nrm((DEPTH, d, MEM_WIDTH), d ** -0.5),
        'w_mv': nrm((DEPTH, d, MEM_WIDTH), d ** -0.5),
        'w_mo': nrm((DEPTH, MEM_WIDTH, d), MEM_WIDTH ** -0.5),
        'g_mem_qn': gain((DEPTH, MEM_HEAD_DIM)),
        'g_mem_kn': gain((DEPTH, MEM_HEAD_DIM)),
        'g_ffn': gain((DEPTH, d)),
        'w_rg': nrm((DEPTH, d, N_GROUPS), d ** -0.5),
        'b_rg': nrm((DEPTH, N_GROUPS), 0.01),
        'w_re': nrm((DEPTH, d, N_EXPERTS), d ** -0.5),
        'b_re': nrm((DEPTH, N_EXPERTS), 0.01),
        'w_gate': nrm((DEPTH, N_EXPERTS, d, D_EXPERT), d ** -0.5),
        'w_up': nrm((DEPTH, N_EXPERTS, d, D_EXPERT), d ** -0.5),
        'w_down': nrm((DEPTH, N_EXPERTS, D_EXPERT, d), D_EXPERT ** -0.5),
    }


def reference(x_prompt, x_sample, mem_prompt, cache_sb_k, cache_sb_v, cache_da_k, cache_da_v,
              cache_mem_k, cache_mem_v, page_table, g_attn, w_in, g_da_qn, g_da_kn,
              lam_q1, lam_k1, lam_q2, lam_k2, g_sb_out, g_da_out, w_o, g_mem_x, g_mem_src,
              w_mq, w_mk, w_mv, w_mo, g_mem_qn, g_mem_kn, g_ffn, w_rg, b_rg, w_re, b_re,
              w_gate, w_up, w_down):
    past_len = page_table.shape[1] * PAGE_SIZE
    dec_seq = x_sample.shape[1]
    k_pos_s = jnp.arange(past_len + dec_seq)
    q_pos_s = past_len + jnp.arange(dec_seq)
    xp, xs = x_prompt, x_sample
    sbk_p, sbv_p, dak_p, dav_p, mk_p, mv_p = [], [], [], [], [], []
    sbk_s, sbv_s, dak_s, dav_s = [], [], [], []
    for l in range(DEPTH):
        lam_init = 0.8 - 0.6 * math.exp(-0.3 * l)
        lam = diff_lambda(lam_q1[l], lam_k1[l], lam_q2[l], lam_k2[l], lam_init)
        da_fn = functools.partial(diff_attend, lam=lam)

        sq, sk, sv, dq, dk, dv = mixer_inputs(rms_norm(xp, g_attn[l]), w_in[l], g_da_qn[l], g_da_kn[l])
        sb_o = sweep_query_blocks(sb_attend, sq, sk, sv)
        da_o = sweep_query_blocks(da_fn, dq, dk, dv)
        xp = xp + merge_heads(sb_o, da_o, g_sb_out[l], g_da_out[l], lam_init, w_o[l])
        mk, mv = memory_kv(mem_prompt, g_mem_src[l], w_mk[l], w_mv[l], g_mem_kn[l])
        xp = xp + memory_attend(rms_norm(xp, g_mem_x[l]), mk, mv, w_mq[l], g_mem_qn[l], w_mo[l])
        xp = xp + hmoe(rms_norm(xp, g_ffn[l]), w_rg[l], b_rg[l], w_re[l], b_re[l], w_gate[l], w_up[l], w_down[l])
        sbk_p.append(sk); sbv_p.append(sv); dak_p.append(dk); dav_p.append(dv); mk_p.append(mk); mv_p.append(mv)

        sq, sk, sv, dq, dk, dv = mixer_inputs(rms_norm(xs, g_attn[l]), w_in[l], g_da_qn[l], g_da_kn[l])
        sb_o = sb_attend(sq, with_past(cache_sb_k, l, page_table, sk), with_past(cache_sb_v, l, page_table, sv), q_pos_s, k_pos_s)
        da_o = diff_attend(dq, with_past(cache_da_k, l, page_table, dk), with_past(cache_da_v, l, page_table, dv), q_pos_s, k_pos_s, lam)
        xs = xs + merge_heads(sb_o, da_o, g_sb_out[l], g_da_out[l], lam_init, w_o[l])
        xs = xs + memory_attend(rms_norm(xs, g_mem_x[l]), cache_mem_k[l], cache_mem_v[l], w_mq[l], g_mem_qn[l], w_mo[l])
        xs = xs + hmoe(rms_norm(xs, g_ffn[l]), w_rg[l], b_rg[l], w_re[l], b_re[l], w_gate[l], w_up[l], w_down[l])
        sbk_s.append(sk); sbv_s.append(sv); dak_s.append(dk); dav_s.append(dv)

    return (xp, xs,
            jnp.stack(sbk_p, axis=0), jnp.stack(sbv_p, axis=0),
            jnp.stack(dak_p, axis=0), jnp.stack(dav_p, axis=0),
            jnp.stack(mk_p, axis=0), jnp.stack(mv_p, axis=0),
            jnp.stack(sbk_s, axis=0), jnp.stack(sbv_s, axis=0),
            jnp.stack(dak_s, axis=0), jnp.stack(dav_s, axis=0))
```

```python
import functools
import math

import jax
import jax.numpy as jnp
from jax import lax
from jax.experimental import pallas as pl
from jax.experimental.pallas import tpu as pltpu

F32 = jnp.float32
BF16 = jnp.bfloat16
RMS_EPS = 1e-6
NEG_INF = float("-inf")

LANES = 128
SB_HEAD_DIM = 64
DA_HEAD_DIM = 64
N_GROUPS = 4
EXPERTS_PER_GROUP = 8
N_EXPERTS = N_GROUPS * EXPERTS_PER_GROUP
EXP_ZERO_ARG = -104.0
VMEM_LIMIT = 56 * 1024 * 1024


def _cparams(sem):
    return pltpu.CompilerParams(dimension_semantics=sem, vmem_limit_bytes=VMEM_LIMIT)


def _block_diag_mean(width, chunk):
    i = jnp.arange(width) // chunk
    return jnp.where(i[:, None] == i[None, :], 1.0 / chunk, 0.0).astype(BF16)


def _cumsum_matrix():
    j = jnp.arange(2 * LANES) % LANES
    s = jnp.arange(2 * LANES)
    return jnp.where((s[None, :] >= LANES) | (j[:, None] > s[None, :]), 1.0, 0.0).astype(BF16)


def _rms(x, g):
    return x * lax.rsqrt(jnp.mean(x * x, axis=-1, keepdims=True) + RMS_EPS) * g


def _chunk_rms(t, bd, g):
    ms = jnp.dot((t * t).astype(BF16), bd, preferred_element_type=F32)
    return t * lax.rsqrt(ms + RMS_EPS) * g


def _split_hi_lo(x):
    hi = x.astype(BF16)
    lo = (x - hi.astype(F32)).astype(BF16)
    return hi, lo


def _softplus(z):
    return jnp.maximum(z, 0.0) + jnp.log(1.0 + jnp.exp(-jnp.abs(z)))


def _in_proj_kernel(x_ref, g_ref, w_ref, gq_ref, gk_ref, bd_ref,
                    sbk_ref, sbv_ref, dak_ref, dav_ref,
                    sbq16_ref, sbk16_ref, sbv16_ref, daq16_ref, dak16_ref, dav16_ref):
    h = _rms(x_ref[...], g_ref[...])
    p = jnp.dot(h.astype(BF16), w_ref[...], preferred_element_type=F32)
    sb_q, sb_k, sb_v = p[:, 0:512], p[:, 512:1024], p[:, 1024:1536]
    da_q = _chunk_rms(p[:, 1536:2048], bd_ref[...], gq_ref[...])
    da_k = _chunk_rms(p[:, 2048:2560], bd_ref[...], gk_ref[...])
    da_v = p[:, 2560:3072]
    sbk_ref[...] = sb_k
    sbv_ref[...] = sb_v
    dak_ref[...] = da_k
    dav_ref[...] = da_v
    sbq16_ref[...] = (sb_q * 0.125).astype(BF16)
    sbk16_ref[...] = sb_k.astype(BF16)
    sbv16_ref[...] = sb_v.astype(BF16)
    daq16_ref[...] = (da_q * 0.125).astype(BF16)
    dak16_ref[...] = da_k.astype(BF16)
    dav16_ref[...] = da_v.astype(BF16)


def _in_proj(x, g, w16, gq, gk, bd64):
    n, d = x.shape
    tm = min(n, 256)
    row = lambda i: (i, 0)
    fix = lambda i: (0, 0)
    o32 = jax.ShapeDtypeStruct((n, 512), F32)
    o16 = jax.ShapeDtypeStruct((n, 512), BF16)
    ospec = pl.BlockSpec((tm, 512), row)
    return pl.pallas_call(
        _in_proj_kernel,
        grid=(n // tm,),
        in_specs=[pl.BlockSpec((tm, d), row), pl.BlockSpec((1, d), fix),
                  pl.BlockSpec(w16.shape, fix), pl.BlockSpec((1, 512), fix),
                  pl.BlockSpec((1, 512), fix), pl.BlockSpec((512, 512), fix)],
        out_specs=[ospec] * 10,
        out_shape=[o32] * 4 + [o16] * 6,
        compiler_params=_cparams(("parallel",)),
        name="in_proj",
    )(x, g, w16, gq, gk, bd64)


def _pair_rows(t, in_a):
    zero = jnp.zeros_like(t)
    return jnp.concatenate([jnp.where(in_a, t, zero), jnp.where(in_a, zero, t)], axis=0)


def _sb_block(q, k, v, u2, valid2, car, in_a):
    z = lax.dot_general(q, _pair_rows(k, in_a), (((1,), (1,)), ((), ())),
                        preferred_element_type=F32)
    sp = _softplus(z)
    l1m = -sp if valid2 is None else jnp.where(valid2, -sp, 0.0)
    hi, lo = _split_hi_lo(l1m)
    r_a = jnp.dot(jnp.concatenate([hi[:, :LANES], lo[:, :LANES]], axis=1), u2,
                  preferred_element_type=F32)
    r_b = jnp.dot(jnp.concatenate([hi[:, LANES:], lo[:, LANES:]], axis=1), u2,
                  preferred_element_type=F32)
    after = jnp.concatenate([r_a[:, :LANES], r_b[:, :LANES]], axis=1)
    total = jnp.concatenate([r_a[:, LANES:], r_b[:, LANES:]], axis=1)
    w = jnp.exp((z - sp) + after + car)
    if valid2 is not None:
        w = jnp.where(valid2, w, 0.0)
    pv = jnp.dot(w.astype(BF16), _pair_rows(v, in_a), preferred_element_type=F32)
    return pv, total


def _sb_attn_kernel(q_ref, k_ref, v_ref, u2_ref, o_ref, acc_ref, car_ref):
    t = LANES
    nq = q_ref.shape[1] // t
    lane = lax.broadcasted_iota(jnp.int32, (t, LANES), 1)
    row = lax.broadcasted_iota(jnp.int32, (t, LANES), 0)
    in_a = lane < SB_HEAD_DIM
    col_minus_row = lane - row

    def q_body(qi, carry):
        q = q_ref[0, pl.ds(pl.multiple_of(qi * t, t), t), :]
        acc_ref[...] = jnp.zeros_like(acc_ref)
        car_ref[...] = jnp.zeros_like(car_ref)

        def cond(st):
            kj, mx = st
            return jnp.logical_and(kj >= 0, mx > EXP_ZERO_ARG)

        def body(st):
            kj, _ = st
            ks = pl.multiple_of(kj * t, t)
            valid = col_minus_row < (qi - kj) * t
            valid2 = jnp.concatenate([valid, valid], axis=1)
            car = car_ref[...]
            pv, total = _sb_block(q, k_ref[0, pl.ds(ks, t), :], v_ref[0, pl.ds(ks, t), :],
                                  u2_ref[...], valid2, car, in_a)
            acc_ref[...] += pv
            car = car + total
            car_ref[...] = car
            return kj - 1, jnp.max(car)

        lax.while_loop(cond, body, (qi, jnp.float32(0.0)))
        o_ref[0, pl.ds(pl.multiple_of(qi * t, t), t), :] = acc_ref[...]
        return carry

    lax.fori_loop(0, nq, q_body, 0)


def _sb_attn(q16, k16, v16, u2):
    b, s, w = q16.shape
    spec = pl.BlockSpec((1, s, LANES), lambda i, p: (i, 0, p))
    return pl.pallas_call(
        _sb_attn_kernel,
        grid=(b, w // LANES),
        in_specs=[spec, spec, spec, pl.BlockSpec(u2.shape, lambda i, p: (0, 0))],
        out_specs=spec,
        out_shape=jax.ShapeDtypeStruct((b, s, w), F32),
        scratch_shapes=[pltpu.VMEM((LANES, LANES), F32), pltpu.VMEM((LANES, 2 * LANES), F32)],
        compiler_params=_cparams(("parallel", "parallel")),
        name="sb_attn",
    )(q16, k16, v16, u2)


def _lam_value(lq1, lk1, lq2, lk2, lam_init):
    f = lambda a, b: jnp.exp(jnp.sum(a * b, axis=-1, keepdims=True))
    return f(lq1, lk1) - f(lq2, lk2) + lam_init


def _da_attn_kernel(nblk_ref, slope_ref, q_ref, k_ref, v_ref, lq1_ref, lk1_ref, lq2_ref, lk2_ref,
                    o_ref, m_ref, l_ref, acc_ref, *, lam_init):
    t = LANES
    nq = q_ref.shape[1] // t
    h = pl.program_id(1)
    slope = slope_ref[h]
    n_max = nblk_ref[h]
    lane = lax.broadcasted_iota(jnp.int32, (t, LANES), 1)
    row = lax.broadcasted_iota(jnp.int32, (t, LANES), 0)
    in_a = lane < DA_HEAD_DIM
    col_minus_row = lane - row
    bias = slope * col_minus_row.astype(F32)
    lam = _lam_value(lq1_ref[...], lk1_ref[...], lq2_ref[...], lk2_ref[...], lam_init)

    def q_body(qi, carry):
        q = q_ref[0, pl.ds(pl.multiple_of(qi * t, t), t), :]
        m_ref[...] = jnp.full_like(m_ref, NEG_INF)
        l_ref[...] = jnp.zeros_like(l_ref)
        acc_ref[...] = jnp.zeros_like(acc_ref)

        def kv_body(step, c):
            kj = qi - step
            ks = pl.multiple_of(kj * t, t)
            k = k_ref[0, pl.ds(ks, t), :]
            v = v_ref[0, pl.ds(ks, t), :]
            s = lax.dot_general(q, _pair_rows(k, in_a), (((1,), (1,)), ((), ())),
                                preferred_element_type=F32)
            off = step * t
            shift = slope * off.astype(F32)
            valid = col_minus_row <= off
            ps, alphas = [], []
            for i in range(2):
                si = jnp.where(valid, s[:, i * LANES:(i + 1) * LANES] + bias, NEG_INF)
                m_prev = m_ref[:, i * LANES:(i + 1) * LANES]
                m_new = jnp.maximum(m_prev, jnp.max(si, axis=1, keepdims=True) - shift)
                p = jnp.exp(si - (m_new + shift))
                alpha = jnp.exp(m_prev - m_new)
                l_ref[:, i * LANES:(i + 1) * LANES] = (
                    alpha * l_ref[:, i * LANES:(i + 1) * LANES] + jnp.sum(p, axis=1, keepdims=True))
                m_ref[:, i * LANES:(i + 1) * LANES] = m_new
                ps.append(p)
                alphas.append(alpha)
            zero = jnp.zeros_like(v)
            v2 = jnp.concatenate([jnp.concatenate([v, zero], axis=1),
                                  jnp.concatenate([zero, v], axis=1)], axis=0)
            pv = jnp.dot(jnp.concatenate(ps, axis=1).astype(BF16), v2, preferred_element_type=F32)
            acc_ref[...] = jnp.concatenate(alphas, axis=1) * acc_ref[...] + pv
            return c

        lax.fori_loop(0, jnp.minimum(qi + 1, n_max), kv_body, 0)
        o = acc_ref[...] / l_ref[...]
        o_ref[0, pl.ds(pl.multiple_of(qi * t, t), t), :] = o[:, :LANES] - lam * o[:, LANES:]
        return carry

    lax.fori_loop(0, nq, q_body, 0)


def _da_attn(q16, k16, v16, nblk, slopes, lams, lam_init):
    b, s, w = q16.shape
    spec = pl.BlockSpec((1, s, LANES), lambda i, p, *_: (i, 0, p))
    lspec = pl.BlockSpec((1, DA_HEAD_DIM), lambda i, p, *_: (0, 0))
    return pl.pallas_call(
        functools.partial(_da_attn_kernel, lam_init=lam_init),
        grid_spec=pltpu.PrefetchScalarGridSpec(
            num_scalar_prefetch=1,
            grid=(b, w // LANES),
            in_specs=[pl.BlockSpec(memory_space=pltpu.SMEM), spec, spec, spec,
                      lspec, lspec, lspec, lspec],
            out_specs=spec,
            scratch_shapes=[pltpu.VMEM((LANES, 2 * LANES), F32), pltpu.VMEM((LANES, 2 * LANES), F32),
                            pltpu.VMEM((LANES, 2 * LANES), F32)],
        ),
        out_shape=jax.ShapeDtypeStruct((b, s, w), F32),
        compiler_params=_cparams(("parallel", "parallel")),
        name="da_attn",
    )(nblk, slopes, q16, k16, v16, *lams)


def _da_block_limits(g_qn, g_kn, slopes):
    bound = 8.0 * jnp.max(jnp.abs(g_qn)) * jnp.max(jnp.abs(g_kn)) * 1.02 + 0.1
    dist = (2.0 * bound - EXP_ZERO_ARG) / slopes
    nblk = jnp.ceil(dist / LANES) + 2.0
    return jnp.minimum(nblk, 2.0 ** 30).astype(jnp.int32)


def _merge_kernel(sb_ref, da_ref, x_ref, gsb_ref, gda_ref, bd64_ref, bd128_ref, wo_ref, o_ref):
    sb = _chunk_rms(sb_ref[...], bd64_ref[...], gsb_ref[...])
    da = _chunk_rms(da_ref[...], bd128_ref[...], gda_ref[...])
    cat = jnp.concatenate([sb, da], axis=1).astype(BF16)
    o_ref[...] = x_ref[...] + jnp.dot(cat, wo_ref[...], preferred_element_type=F32)


def _merge_out(sb_o, da_o, x, gsb, gda, bd64, bd128, wo16):
    n, d = x.shape
    tm = min(n, 512)
    row = lambda i: (i, 0)
    fix = lambda i: (0, 0)
    return pl.pallas_call(
        _merge_kernel,
        grid=(n // tm,),
        in_specs=[pl.BlockSpec((tm, 512), row), pl.BlockSpec((tm, 512), row), pl.BlockSpec((tm, d), row),
                  pl.BlockSpec((1, 512), fix), pl.BlockSpec((1, 512), fix),
                  pl.BlockSpec((512, 512), fix), pl.BlockSpec((512, 512), fix),
                  pl.BlockSpec(wo16.shape, fix)],
        out_specs=pl.BlockSpec((tm, d), row),
        out_shape=jax.ShapeDtypeStruct((n, d), F32),
        compiler_params=_cparams(("parallel",)),
        name="merge_out",
    )(sb_o, da_o, x, gsb, gda, bd64, bd128, wo16)


def _mem_kv_kernel(x_ref, g_ref, w_ref, gk_ref, bd_ref, k_ref, v_ref):
    m = _rms(x_ref[...], g_ref[...])
    p = jnp.dot(m.astype(BF16), w_ref[...], preferred_element_type=F32)
    k_ref[...] = _chunk_rms(p[:, :512], bd_ref[...], gk_ref[...])
    v_ref[...] = p[:, 512:]


def _mem_kv(mem, g, wkv16, gk, bd128):
    n, d = mem.shape
    tm = min(n, 256)
    row = lambda i: (i, 0)
    fix = lambda i: (0, 0)
    o = jax.ShapeDtypeStruct((n, 512), F32)
    return pl.pallas_call(
        _mem_kv_kernel,
        grid=(n // tm,),
        in_specs=[pl.BlockSpec((tm, d), row), pl.BlockSpec((1, d), fix), pl.BlockSpec(wkv16.shape, fix),
                  pl.BlockSpec((1, 512), fix), pl.BlockSpec((512, 512), fix)],
        out_specs=[pl.BlockSpec((tm, 512), row)] * 2,
        out_shape=[o, o],
        compiler_params=_cparams(("parallel",)),
        name="mem_kv",
    )(mem, g, wkv16, gk, bd128)


def _mem_attn_kernel(x_ref, mk_ref, mv_ref, g_ref, wq_ref, gq_ref, bd_ref, wo_ref, o_ref):
    tm = x_ref.shape[1]
    x = x_ref[0]
    if tm < 8:
        x = jnp.broadcast_to(x[:1], (8, x.shape[1]))
    h = _rms(x, g_ref[...])
    q = jnp.dot(h.astype(BF16), wq_ref[...], preferred_element_type=F32)
    q = (_chunk_rms(q, bd_ref[...], gq_ref[...]) * (LANES ** -0.5)).astype(BF16)
    mk = mk_ref[0].astype(BF16)
    mv = mv_ref[0].astype(BF16)
    outs = []
    for hd in range(4):
        sl = slice(hd * LANES, (hd + 1) * LANES)
        s = lax.dot_general(q[:, sl], mk[:, sl], (((1,), (1,)), ((), ())), preferred_element_type=F32)
        p = jnp.exp(s - jnp.max(s, axis=1, keepdims=True))
        p = p / jnp.sum(p, axis=1, keepdims=True)
        outs.append(jnp.dot(p.astype(BF16), mv[:, sl], preferred_element_type=F32))
    o = jnp.concatenate(outs, axis=1).astype(BF16)
    o_ref[0] = (x + jnp.dot(o, wo_ref[...], preferred_element_type=F32))[:tm]


def _mem_attn(x, mk, mv, g, wq16, gq, bd128, wo16):
    b, t, d = x.shape
    tm = min(t, 512)
    m = mk.shape[1]
    fix = lambda i, j: (0, 0)
    return pl.pallas_call(
        _mem_attn_kernel,
        grid=(b, t // tm),
        in_specs=[pl.BlockSpec((1, tm, d), lambda i, j: (i, j, 0)),
                  pl.BlockSpec((1, m, 512), lambda i, j: (i, 0, 0)),
                  pl.BlockSpec((1, m, 512), lambda i, j: (i, 0, 0)),
                  pl.BlockSpec((1, d), fix), pl.BlockSpec(wq16.shape, fix), pl.BlockSpec((1, 512), fix),
                  pl.BlockSpec((512, 512), fix), pl.BlockSpec(wo16.shape, fix)],
        out_specs=pl.BlockSpec((1, tm, d), lambda i, j: (i, j, 0)),
        out_shape=jax.ShapeDtypeStruct((b, t, d), F32),
        compiler_params=_cparams(("parallel", "parallel")),
        name="mem_attn",
    )(x, mk, mv, g, wq16, gq, bd128, wo16)


def _first_max(vals, lane_f):
    mx = jnp.max(vals, axis=1, keepdims=True)
    idx = jnp.min(jnp.where(vals == mx, lane_f, float(LANES)), axis=1, keepdims=True)
    return mx, idx


def _router_kernel(x_ref, g_ref, whi_ref, wlo_ref, b_ref, h16_ref, comb_ref):
    h = _rms(x_ref[...], g_ref[...])
    hi, lo = _split_hi_lo(h)
    h16_ref[...] = hi
    lg = (jnp.dot(hi, whi_ref[...], preferred_element_type=F32)
          + jnp.dot(lo, whi_ref[...], preferred_element_type=F32)
          + jnp.dot(hi, wlo_ref[...], preferred_element_type=F32)) + b_ref[...]
    lane = lax.broadcasted_iota(jnp.int32, lg.shape, 1)
    lane_f = lane.astype(F32)
    is_g = lane < N_GROUPS
    gl = jnp.where(is_g, lg, NEG_INF)
    g_max, g_sel = _first_max(gl, lane_f)
    p_g = 1.0 / jnp.sum(jnp.where(is_g, jnp.exp(gl - g_max), 0.0), axis=1, keepdims=True)
    e_lo = N_GROUPS + g_sel * EXPERTS_PER_GROUP
    is_e = jnp.logical_and(lane_f >= e_lo, lane_f < e_lo + EXPERTS_PER_GROUP)
    el = jnp.where(is_e, lg, NEG_INF)
    t1, i1 = _first_max(el, lane_f)
    t2, i2 = _first_max(jnp.where(lane_f == i1, NEG_INF, el), lane_f)
    e2 = jnp.exp(t2 - t1)
    w1 = p_g / (1.0 + e2)
    comb_ref[...] = jnp.where(lane_f == i1, w1, 0.0) + jnp.where(lane_f == i2, w1 * e2, 0.0)


def _router(x, g, whi, wlo, bias):
    n, d = x.shape
    tm = min(n, 512)
    row = lambda i: (i, 0)
    fix = lambda i: (0, 0)
    return pl.pallas_call(
        _router_kernel,
        grid=(n // tm,),
        in_specs=[pl.BlockSpec((tm, d), row), pl.BlockSpec((1, d), fix),
                  pl.BlockSpec((d, LANES), fix), pl.BlockSpec((d, LANES), fix), pl.BlockSpec((1, LANES), fix)],
        out_specs=[pl.BlockSpec((tm, d), row), pl.BlockSpec((tm, LANES), row)],
        out_shape=[jax.ShapeDtypeStruct((n, d), BF16), jax.ShapeDtypeStruct((n, LANES), F32)],
        compiler_params=_cparams(("parallel",)),
        name="router",
    )(x, g, whi, wlo, bias)


def _moe_dense_kernel(h_ref, comb_ref, res_ref, wg_ref, wu_ref, wd_ref, o_ref):
    e = pl.program_id(1)

    @pl.when(e == 0)
    def _():
        o_ref[...] = res_ref[...]

    h = h_ref[...]
    g = jnp.dot(h, wg_ref[0], preferred_element_type=F32)
    u = jnp.dot(h, wu_ref[0], preferred_element_type=F32)
    lane = lax.broadcasted_iota(jnp.int32, comb_ref.shape, 1)
    c = jnp.sum(jnp.where(lane == e + N_GROUPS, comb_ref[...], 0.0), axis=1, keepdims=True)
    a = (g * jax.nn.sigmoid(g)) * u * c
    o_ref[...] += jnp.dot(a.astype(BF16), wd_ref[0], preferred_element_type=F32)


def _moe_dense(h16, comb, res, wg16, wu16, wd16):
    n, d = res.shape
    f = wg16.shape[2]
    tm = min(n, 1024)
    row = lambda i, e: (i, 0)
    return pl.pallas_call(
        _moe_dense_kernel,
        grid=(n // tm, N_EXPERTS),
        in_specs=[pl.BlockSpec((tm, d), row), pl.BlockSpec((tm, LANES), row), pl.BlockSpec((tm, d), row),
                  pl.BlockSpec((1, d, f), lambda i, e: (e, 0, 0)),
                  pl.BlockSpec((1, d, f), lambda i, e: (e, 0, 0)),
                  pl.BlockSpec((1, f, d), lambda i, e: (e, 0, 0))],
        out_specs=pl.BlockSpec((tm, d), row),
        out_shape=jax.ShapeDtypeStruct((n, d), F32),
        compiler_params=_cparams(("parallel", "arbitrary")),
        name="moe_dense",
    )(h16, comb, res, wg16, wu16, wd16)


PAGES_PER_STEP = 8


def _head_row_mask(width, lanes_per_row):
    lane = lax.broadcasted_iota(jnp.int32, (8, width), 1)
    row = lax.broadcasted_iota(jnp.int32, (8, width), 0)
    return jnp.logical_and(lane >= row * lanes_per_row, lane < (row + 1) * lanes_per_row)


def _sb_decode_kernel(pt_ref, q_ref, u2_ref, *refs):
    npg = PAGES_PER_STEP
    k_refs, v_refs = refs[:npg], refs[npg:2 * npg]
    o_ref, acc_ref, car_ref, done_ref = refs[2 * npg:]
    j = pl.program_id(1)

    @pl.when(j == 0)
    def _():
        acc_ref[...] = jnp.zeros_like(acc_ref)
        car_ref[...] = jnp.zeros_like(car_ref)
        done_ref[0] = 0

    @pl.when(done_ref[0] == 0)
    def _():
        width = q_ref.shape[2]
        q = jnp.broadcast_to(q_ref[0].astype(F32), (8, width))
        qh = jnp.where(_head_row_mask(width, SB_HEAD_DIM), q, 0.0).astype(BF16)
        car = car_ref[...]
        acc = acc_ref[...]
        for i in range(npg):
            kt = k_refs[i][0, 0].reshape(width, LANES).astype(BF16)
            vt = v_refs[i][0, 0].reshape(width, LANES).astype(BF16)
            z = jnp.dot(qh, kt, preferred_element_type=F32)
            sp = _softplus(z)
            hi, lo = _split_hi_lo(-sp)
            r = jnp.dot(jnp.concatenate([hi, lo], axis=1), u2_ref[...], preferred_element_type=F32)
            w = jnp.exp((z - sp) + r[:, :LANES] + car)
            acc = acc + lax.dot_general(w.astype(BF16), vt, (((1,), (1,)), ((), ())),
                                        preferred_element_type=F32)
            car = car + r[:, LANES:]
        acc_ref[...] = acc
        car_ref[...] = car
        done_ref[0] = (jnp.max(car) <= EXP_ZERO_ARG).astype(jnp.int32)

    @pl.when(j == pl.num_programs(1) - 1)
    def _():
        keep = _head_row_mask(q_ref.shape[2], SB_HEAD_DIM)
        o_ref[0] = jnp.sum(jnp.where(keep, acc_ref[...], 0.0), axis=0, keepdims=True)


def _page_specs(layer, n_pages, page_shape, reverse):
    specs = []
    for i in range(PAGES_PER_STEP):
        def imap(b, j, pt, i=i):
            slot = j * PAGES_PER_STEP + i
            if reverse:
                slot = n_pages - 1 - slot
            return (layer, pt[b * n_pages + slot], 0, 0, 0)
        specs.append(pl.BlockSpec((1, 1) + tuple(page_shape), imap))
    return specs


def _sb_decode(q, cache_kt, cache_vt, pt_flat, u2, layer, n_pages):
    b, _, width = q.shape
    assert cache_kt.shape[4] == LANES and cache_kt.shape[2] * cache_kt.shape[3] == width
    qspec = pl.BlockSpec((1, 1, width), lambda i, j, pt: (i, 0, 0))
    pages = _page_specs(layer, n_pages, cache_kt.shape[2:], True)
    return pl.pallas_call(
        _sb_decode_kernel,
        grid_spec=pltpu.PrefetchScalarGridSpec(
            num_scalar_prefetch=1,
            grid=(b, n_pages // PAGES_PER_STEP),
            in_specs=[qspec, pl.BlockSpec(u2.shape, lambda i, j, pt: (0, 0))] + pages + pages,
            out_specs=qspec,
            scratch_shapes=[pltpu.VMEM((8, width), F32), pltpu.VMEM((8, LANES), F32),
                            pltpu.SMEM((1,), jnp.int32)],
        ),
        out_shape=jax.ShapeDtypeStruct((b, 1, width), F32),
        compiler_params=_cparams(("parallel", "arbitrary")),
        name="sb_decode",
    )(pt_flat, q, u2, *([cache_kt] * PAGES_PER_STEP), *([cache_vt] * PAGES_PER_STEP))


def _da_rows(vec):
    w = vec.shape[1]
    lane = lax.broadcasted_iota(jnp.int32, (8, w), 1)
    row = lax.broadcasted_iota(jnp.int32, (8, w), 0)
    start = (row % 4) * LANES + (row // 4) * DA_HEAD_DIM
    keep = jnp.logical_and(lane >= start, lane < start + DA_HEAD_DIM)
    return jnp.where(keep, jnp.broadcast_to(vec, (8, w)), 0.0)


def _da_decode_kernel(pt_ref, q_ref, kn_ref, vn_ref, lq1_ref, lk1_ref, lq2_ref, lk2_ref, *refs,
                      lam_init, past_len):
    npg = PAGES_PER_STEP
    k_refs, v_refs = refs[:npg], refs[npg:2 * npg]
    o_ref, m_ref, l_ref, acc_ref = refs[2 * npg:]
    j = pl.program_id(1)
    width = q_ref.shape[2]
    qh = _da_rows(q_ref[0].astype(F32))
    row = lax.broadcasted_iota(jnp.int32, (8, LANES), 0)
    lane = lax.broadcasted_iota(jnp.int32, (8, LANES), 1)
    slope = jnp.exp2(-2.0 * ((row % 4) + 1).astype(F32))

    @pl.when(j == 0)
    def _():
        s_self = jnp.sum(qh * jnp.broadcast_to(kn_ref[0], (8, width)), axis=1, keepdims=True)
        m_ref[...] = jnp.broadcast_to(s_self, m_ref.shape)
        l_ref[...] = jnp.ones_like(l_ref)
        acc_ref[...] = jnp.broadcast_to(vn_ref[0], acc_ref.shape)

    q16 = qh.astype(BF16)
    m = m_ref[...]
    l = l_ref[...]
    acc = acc_ref[...]
    n_heads = k_refs[0].shape[3]
    for i in range(npg):
        s = jnp.zeros((8, LANES), F32)
        for hd in range(n_heads):
            k = k_refs[i][0, 0, :, hd, :].astype(BF16)
            s = s + lax.dot_general(q16[:, hd * LANES:(hd + 1) * LANES], k, (((1,), (1,)), ((), ())),
                                    preferred_element_type=F32)
        k_pos = (j * npg + i) * LANES + lane
        s = s - slope * (past_len - k_pos).astype(F32)
        m_new = jnp.maximum(m, jnp.max(s, axis=1, keepdims=True))
        p = jnp.exp(s - m_new)
        alpha = jnp.exp(m - m_new)
        l = alpha * l + jnp.sum(p, axis=1, keepdims=True)
        p16 = p.astype(BF16)
        pv = [jnp.dot(p16, v_refs[i][0, 0, :, hd, :].astype(BF16), preferred_element_type=F32)
              for hd in range(n_heads)]
        acc = alpha[:, :1] * acc + jnp.concatenate(pv, axis=1)
        m = m_new
    m_ref[...] = m
    l_ref[...] = l
    acc_ref[...] = acc

    @pl.when(j == pl.num_programs(1) - 1)
    def _():
        lam = _lam_value(lq1_ref[...], lk1_ref[...], lq2_ref[...], lk2_ref[...], lam_init)
        lane_w = lax.broadcasted_iota(jnp.int32, (8, width), 1)
        row_w = lax.broadcasted_iota(jnp.int32, (8, width), 0)
        keep = (lane_w // LANES) == (row_w % 4)
        coef = jnp.where(row_w < 4, 1.0, -lam)
        o = acc / l[:, :1] * coef
        o_ref[0] = jnp.sum(jnp.where(keep, o, 0.0), axis=0, keepdims=True)


def _da_decode(q, k_new, v_new, cache_k, cache_v, pt_flat, lams, layer, n_pages, lam_init):
    b, _, width = q.shape
    rows = cache_k.shape[2]
    assert cache_k.shape[4] == LANES and cache_k.shape[3] * LANES == width and rows == LANES
    qspec = pl.BlockSpec((1, 1, width), lambda i, j, pt: (i, 0, 0))
    lspec = pl.BlockSpec((1, DA_HEAD_DIM), lambda i, j, pt: (0, 0))
    pages = _page_specs(layer, n_pages, cache_k.shape[2:], False)
    return pl.pallas_call(
        functools.partial(_da_decode_kernel, lam_init=lam_init, past_len=n_pages * rows),
        grid_spec=pltpu.PrefetchScalarGridSpec(
            num_scalar_prefetch=1,
            grid=(b, n_pages // PAGES_PER_STEP),
            in_specs=[qspec, qspec, qspec, lspec, lspec, lspec, lspec] + pages + pages,
            out_specs=qspec,
            scratch_shapes=[pltpu.VMEM((8, LANES), F32), pltpu.VMEM((8, LANES), F32),
                            pltpu.VMEM((8, width), F32)],
        ),
        out_shape=jax.ShapeDtypeStruct((b, 1, width), F32),
        compiler_params=_cparams(("parallel", "arbitrary")),
        name="da_decode",
    )(pt_flat, q, k_new, v_new, *lams, *([cache_k] * PAGES_PER_STEP), *([cache_v] * PAGES_PER_STEP))


def _tile_gain(g, reps):
    return jnp.tile(g.astype(F32), reps)[None, :]


def kernel(x_prompt, x_sample, mem_prompt, cache_sb_k, cache_sb_v, cache_da_k, cache_da_v, cache_mem_k, cache_mem_v, page_table, g_attn, w_in, g_da_qn, g_da_kn, lam_q1, lam_k1, lam_q2, lam_k2, g_sb_out, g_da_out, w_o, g_mem_x, g_mem_src, w_mq, w_mk, w_mv, w_mo, g_mem_qn, g_mem_kn, g_ffn, w_rg, b_rg, w_re, b_re, w_gate, w_up, w_down):
    depth = w_in.shape[0]
    bp, sp, d = x_prompt.shape
    bs, ss, _ = x_sample.shape
    assert ss == 1, "the sample group decodes one new token per sequence"
    n_pages, page = page_table.shape[1], cache_sb_k.shape[2]
    n_mem = mem_prompt.shape[1]

    bd64 = _block_diag_mean(512, 64)
    bd128 = _block_diag_mean(512, 128)
    u2 = _cumsum_matrix()
    slopes = jnp.exp2(-8.0 * jnp.arange(1, 5, dtype=F32) / 4)
    pt_flat = page_table.reshape(-1).astype(jnp.int32)
    csbk = jnp.transpose(cache_sb_k, (0, 1, 3, 4, 2))
    csbv = jnp.transpose(cache_sb_v, (0, 1, 3, 4, 2))
    cdak, cdav = cache_da_k, cache_da_v

    xp = x_prompt.reshape(bp * sp, d)
    xs = x_sample.reshape(bs, d)
    mem = mem_prompt.reshape(bp * n_mem, d)
    outs = [[] for _ in range(10)]

    for l in range(depth):
        lam_init = 0.8 - 0.6 * math.exp(-0.3 * l)
        row = lambda v: v[l].astype(F32)[None, :]
        w_in16 = w_in[l].astype(BF16)
        w_o16 = w_o[l].astype(BF16)
        w_mq16, w_mo16 = w_mq[l].astype(BF16), w_mo[l].astype(BF16)
        w_mkv16 = jnp.concatenate([w_mk[l], w_mv[l]], axis=1).astype(BF16)
        wg16, wu16, wd16 = w_gate[l].astype(BF16), w_up[l].astype(BF16), w_down[l].astype(BF16)
        w_r = jnp.zeros((d, LANES), F32).at[:, :N_GROUPS].set(w_rg[l]).at[:, N_GROUPS:N_GROUPS + N_EXPERTS].set(w_re[l])
        w_r_hi = w_r.astype(BF16)
        w_r_lo = (w_r - w_r_hi.astype(F32)).astype(BF16)
        b_r = jnp.zeros((1, LANES), F32).at[0, :N_GROUPS].set(b_rg[l]).at[0, N_GROUPS:N_GROUPS + N_EXPERTS].set(b_re[l])
        gq, gk = _tile_gain(g_da_qn[l], 8), _tile_gain(g_da_kn[l], 8)
        gsb = _tile_gain(g_sb_out[l], 8)
        gda = _tile_gain(g_da_out[l], 4) * (1.0 - lam_init)
        gmq, gmk = _tile_gain(g_mem_qn[l], 4), _tile_gain(g_mem_kn[l], 4)
        lams = (row(lam_q1), row(lam_k1), row(lam_q2), row(lam_k2))
        nblk = _da_block_limits(g_da_qn[l], g_da_kn[l], slopes)

        def ffn(x):
            h16, comb = _router(x, row(g_ffn), w_r_hi, w_r_lo, b_r)
            return _moe_dense(h16, comb, x, wg16, wu16, wd16)

        (sbk, sbv, dak, dav, sbq16, sbk16, sbv16, daq16, dak16, dav16) = _in_proj(
            xp, row(g_attn), w_in16, gq, gk, bd64)
        as3 = lambda t: t.reshape(bp, sp, 512)
        sb_o = _sb_attn(as3(sbq16), as3(sbk16), as3(sbv16), u2)
        da_o = _da_attn(as3(daq16), as3(dak16), as3(dav16), nblk, slopes, lams, lam_init)
        xp = _merge_out(sb_o.reshape(-1, 512), da_o.reshape(-1, 512), xp, gsb, gda, bd64, bd128, w_o16)
        mk, mv = _mem_kv(mem, row(g_mem_src), w_mkv16, gmk, bd128)
        xp = _mem_attn(xp.reshape(bp, sp, d), mk.reshape(bp, n_mem, 512), mv.reshape(bp, n_mem, 512),
                       row(g_mem_x), w_mq16, gmq, bd128, w_mo16).reshape(-1, d)
        xp = ffn(xp)
        for dst, val in zip(outs[:6], (sbk, sbv, dak, dav, mk, mv)):
            dst.append(val)

        (sbk_s, sbv_s, dak_s, dav_s, sbq16, _, _, daq16, _, _) = _in_proj(xs, row(g_attn), w_in16, gq, gk, bd64)
        as1 = lambda t: t.reshape(bs, 1, 512)
        sb_o = _sb_decode(as1(sbq16), csbk, csbv, pt_flat, u2, l, n_pages)
        da_o = _da_decode(as1(daq16), as1(dak_s), as1(dav_s), cdak, cdav, pt_flat, lams, l, n_pages, lam_init)
        xs = _merge_out(sb_o.reshape(bs, 512), da_o.reshape(bs, 512), xs, gsb, gda, bd64, bd128, w_o16)
        xs = _mem_attn(xs.reshape(bs, 1, d), cache_mem_k[l].reshape(bs, n_mem, 512),
                       cache_mem_v[l].reshape(bs, n_mem, 512),
                       row(g_mem_x), w_mq16, gmq, bd128, w_mo16).reshape(bs, d)
        xs = ffn(xs)
        for dst, val in zip(outs[6:], (sbk_s, sbv_s, dak_s, dav_s)):
            dst.append(val)

    st = lambda lst, shape: jnp.stack(lst, axis=0).reshape(shape)
    return (xp.reshape(bp, sp, d), xs.reshape(bs, 1, d),
            st(outs[0], (depth, bp, sp, 8, 64)), st(outs[1], (depth, bp, sp, 8, 64)),
            st(outs[2], (depth, bp, sp, 4, 128)), st(outs[3], (depth, bp, sp, 4, 128)),
            st(outs[4], (depth, bp, n_mem, 4, 128)), st(outs[5], (depth, bp, n_mem, 4, 128)),
            st(outs[6], (depth, bs, 1, 8, 64)), st(outs[7], (depth, bs, 1, 8, 64)),
            st(outs[8], (depth, bs, 1, 4, 128)), st(outs[9], (depth, bs, 1, 4, 128)))
```

```python
import functools
import math

import jax
import jax.numpy as jnp
from jax import lax
from jax.experimental import pallas as pl
from jax.experimental.pallas import tpu as pltpu

F32 = jnp.float32
BF16 = jnp.bfloat16
RMS_EPS = 1e-6
NEG_INF = float("-inf")

LANES = 128
SB_HEAD_DIM = 64
DA_HEAD_DIM = 64
N_GROUPS = 4
EXPERTS_PER_GROUP = 8
N_EXPERTS = N_GROUPS * EXPERTS_PER_GROUP
EXP_ZERO_ARG = -104.0
VMEM_LIMIT = 56 * 1024 * 1024


def _cparams(sem):
    return pltpu.CompilerParams(dimension_semantics=sem, vmem_limit_bytes=VMEM_LIMIT)


def _block_diag_mean(width, chunk):
    i = jnp.arange(width) // chunk
    return jnp.where(i[:, None] == i[None, :], 1.0 / chunk, 0.0).astype(BF16)


def _cumsum_matrix():
    j = jnp.arange(2 * LANES) % LANES
    s = jnp.arange(2 * LANES)
    return jnp.where((s[None, :] >= LANES) | (j[:, None] > s[None, :]), 1.0, 0.0).astype(BF16)


def _rms(x, g):
    return x * lax.rsqrt(jnp.mean(x * x, axis=-1, keepdims=True) + RMS_EPS) * g


def _chunk_rms(t, bd, g):
    ms = jnp.dot((t * t).astype(BF16), bd, preferred_element_type=F32)
    return t * lax.rsqrt(ms + RMS_EPS) * g


def _split_hi_lo(x):
    hi = x.astype(BF16)
    lo = (x - hi.astype(F32)).astype(BF16)
    return hi, lo


def _softplus(z):
    return jnp.maximum(z, 0.0) + jnp.log(1.0 + jnp.exp(-jnp.abs(z)))


def _in_proj_kernel(x_ref, g_ref, w_ref, gq_ref, gk_ref, bd_ref,
                    sbk_ref, sbv_ref, dak_ref, dav_ref,
                    sbq16_ref, sbk16_ref, sbv16_ref, daq16_ref, dak16_ref, dav16_ref):
    h = _rms(x_ref[...], g_ref[...])
    p = jnp.dot(h.astype(BF16), w_ref[...], preferred_element_type=F32)
    sb_q, sb_k, sb_v = p[:, 0:512], p[:, 512:1024], p[:, 1024:1536]
    da_q = _chunk_rms(p[:, 1536:2048], bd_ref[...], gq_ref[...])
    da_k = _chunk_rms(p[:, 2048:2560], bd_ref[...], gk_ref[...])
    da_v = p[:, 2560:3072]
    sbk_ref[...] = sb_k
    sbv_ref[...] = sb_v
    dak_ref[...] = da_k
    dav_ref[...] = da_v
    sbq16_ref[...] = (sb_q * 0.125).astype(BF16)
    sbk16_ref[...] = sb_k.astype(BF16)
    sbv16_ref[...] = sb_v.astype(BF16)
    daq16_ref[...] = (da_q * 0.125).astype(BF16)
    dak16_ref[...] = da_k.astype(BF16)
    dav16_ref[...] = da_v.astype(BF16)


def _in_proj(x, g, w16, gq, gk, bd64):
    n, d = x.shape
    tm = min(n, 256)
    row = lambda i: (i, 0)
    fix = lambda i: (0, 0)
    o32 = jax.ShapeDtypeStruct((n, 512), F32)
    o16 = jax.ShapeDtypeStruct((n, 512), BF16)
    ospec = pl.BlockSpec((tm, 512), row)
    return pl.pallas_call(
        _in_proj_kernel,
        grid=(n // tm,),
        in_specs=[pl.BlockSpec((tm, d), row), pl.BlockSpec((1, d), fix),
                  pl.BlockSpec(w16.shape, fix), pl.BlockSpec((1, 512), fix),
                  pl.BlockSpec((1, 512), fix), pl.BlockSpec((512, 512), fix)],
        out_specs=[ospec] * 10,
        out_shape=[o32] * 4 + [o16] * 6,
        compiler_params=_cparams(("parallel",)),
        name="in_proj",
    )(x, g, w16, gq, gk, bd64)


def _pair_rows(t, in_a):
    zero = jnp.zeros_like(t)
    return jnp.concatenate([jnp.where(in_a, t, zero), jnp.where(in_a, zero, t)], axis=0)


def _sb_block(q, k, v, u2, valid2, car, in_a):
    z = lax.dot_general(q, _pair_rows(k, in_a), (((1,), (1,)), ((), ())),
                        preferred_element_type=F32)
    sp = _softplus(z)
    l1m = -sp if valid2 is None else jnp.where(valid2, -sp, 0.0)
    hi, lo = _split_hi_lo(l1m)
    r_a = jnp.dot(jnp.concatenate([hi[:, :LANES], lo[:, :LANES]], axis=1), u2,
                  preferred_element_type=F32)
    r_b = jnp.dot(jnp.concatenate([hi[:, LANES:], lo[:, LANES:]], axis=1), u2,
                  preferred_element_type=F32)
    after = jnp.concatenate([r_a[:, :LANES], r_b[:, :LANES]], axis=1)
    total = jnp.concatenate([r_a[:, LANES:], r_b[:, LANES:]], axis=1)
    w = jnp.exp((z - sp) + after + car)
    if valid2 is not None:
        w = jnp.where(valid2, w, 0.0)
    pv = jnp.dot(w.astype(BF16), _pair_rows(v, in_a), preferred_element_type=F32)
    return pv, total


SB_Q_CHUNK = 1024


def _sb_attn_kernel(q_ref, k_ref, v_ref, u2_ref, o_ref, acc_ref, car_ref):
    t = LANES
    n_pairs = q_ref.shape[2] // LANES
    nq = q_ref.shape[1] // t
    chunk = pl.program_id(1)
    lane = lax.broadcasted_iota(jnp.int32, (t, LANES), 1)
    row = lax.broadcasted_iota(jnp.int32, (t, LANES), 0)
    in_a = lane < SB_HEAD_DIM
    col_minus_row = lane - row

    def q_body(qq, carry):
        qi = chunk * nq + qq
        qs = pl.ds(pl.multiple_of(qq * t, t), t)
        acc_ref[...] = jnp.zeros_like(acc_ref)
        car_ref[...] = jnp.zeros_like(car_ref)

        def cond(st):
            kj, mx = st
            return jnp.logical_and(kj >= 0, mx > EXP_ZERO_ARG)

        def body(st):
            kj, _ = st
            ks = pl.ds(pl.multiple_of(kj * t, t), t)
            valid = col_minus_row < (qi - kj) * t
            valid2 = jnp.concatenate([valid, valid], axis=1)
            mx = None
            for p in range(n_pairs):
                ln = slice(p * LANES, (p + 1) * LANES)
                car = car_ref[p]
                pv, total = _sb_block(q_ref[0, qs, ln], k_ref[0, ks, ln], v_ref[0, ks, ln],
                                      u2_ref[...], valid2, car, in_a)
                acc_ref[p] += pv
                car = car + total
                car_ref[p] = car
                mx = jnp.max(car) if mx is None else jnp.maximum(mx, jnp.max(car))
            return kj - 1, mx

        lax.while_loop(cond, body, (qi, jnp.float32(0.0)))
        o_ref[0, qs, :] = jnp.concatenate([acc_ref[p] for p in range(n_pairs)], axis=1)
        return carry

    lax.fori_loop(0, nq, q_body, 0)


def _sb_attn(q16, k16, v16, u2):
    b, s, w = q16.shape
    qc = min(s, SB_Q_CHUNK)
    n_pairs = w // LANES
    qspec = pl.BlockSpec((1, qc, w), lambda i, c: (i, c, 0))
    kspec = pl.BlockSpec((1, s, w), lambda i, c: (i, 0, 0))
    return pl.pallas_call(
        _sb_attn_kernel,
        grid=(b, s // qc),
        in_specs=[qspec, kspec, kspec, pl.BlockSpec(u2.shape, lambda i, c: (0, 0))],
        out_specs=qspec,
        out_shape=jax.ShapeDtypeStruct((b, s, w), F32),
        scratch_shapes=[pltpu.VMEM((n_pairs, LANES, LANES), F32),
                        pltpu.VMEM((n_pairs, LANES, 2 * LANES), F32)],
        compiler_params=_cparams(("parallel", "arbitrary")),
        name="sb_attn",
    )(q16, k16, v16, u2)


def _lam_value(lq1, lk1, lq2, lk2, lam_init):
    f = lambda a, b: jnp.exp(jnp.sum(a * b, axis=-1, keepdims=True))
    return f(lq1, lk1) - f(lq2, lk2) + lam_init


DA_TQ = 256
DA_TK = 512


def _da_attn_kernel(nwin_ref, slope_ref, q_ref, k_ref, v_ref, lq1_ref, lk1_ref, lq2_ref, lk2_ref,
                    o_ref, bias_ref, m_ref, l_ref, acc_ref, *, lam_init):
    tq = min(DA_TQ, q_ref.shape[1])
    tk = min(DA_TK, q_ref.shape[1])
    nq = q_ref.shape[1] // tq
    h = pl.program_id(1)
    slope = slope_ref[h]
    n_max = nwin_ref[h]
    lane = lax.broadcasted_iota(jnp.int32, (tq, tk), 1)
    row = lax.broadcasted_iota(jnp.int32, (tq, tk), 0)
    col_minus_row = lane - row
    bias_ref[...] = slope * col_minus_row.astype(F32)
    in_a = lax.broadcasted_iota(jnp.int32, (tq, LANES), 1) < DA_HEAD_DIM
    lam = _lam_value(lq1_ref[...], lk1_ref[...], lq2_ref[...], lk2_ref[...], lam_init)
    nt_dims = (((1,), (1,)), ((), ()))

    def q_body(qi, carry):
        q0 = pl.multiple_of(qi * tq, tq)
        end = q0 + tq
        q = q_ref[0, pl.ds(q0, tq), :]
        zero = jnp.zeros_like(q)
        q_maps = (jnp.where(in_a, q, zero), jnp.where(in_a, zero, q))
        m_ref[...] = jnp.full_like(m_ref, NEG_INF)
        l_ref[...] = jnp.zeros_like(l_ref)
        acc_ref[...] = jnp.zeros_like(acc_ref)

        def window(t, c):
            hi = end - t * tk
            start = pl.multiple_of(jnp.maximum(hi - tk, 0), tq)
            k = k_ref[0, pl.ds(start, tk), :]
            v = v_ref[0, pl.ds(start, tk), :]
            shift = slope * (q0 - start).astype(F32)

            def update(valid):
                for i in range(2):
                    ln = slice(i * LANES, (i + 1) * LANES)
                    s = lax.dot_general(q_maps[i], k, nt_dims, preferred_element_type=F32) + bias_ref[...]
                    if valid is not None:
                        s = jnp.where(valid, s, NEG_INF)
                    m_prev = m_ref[:, ln]
                    m_new = jnp.maximum(m_prev, jnp.max(s, axis=1, keepdims=True) - shift)
                    p = jnp.exp(s - (m_new[:, :1] + shift))
                    alpha = jnp.exp(m_prev - m_new)
                    l_ref[:, ln] = alpha * l_ref[:, ln] + jnp.sum(p, axis=1, keepdims=True)
                    m_ref[:, ln] = m_new
                    acc_ref[:, ln] = alpha * acc_ref[:, ln] + jnp.dot(
                        p.astype(BF16), v, preferred_element_type=F32)

            def masked():
                update(jnp.logical_and(col_minus_row <= q0 - start, lane < hi - start))

            lax.cond(jnp.logical_or(t == 0, hi < tk), masked, lambda: update(None))
            return c

        n_all = lax.div(end + (tk - 1), tk)
        lax.fori_loop(0, jnp.minimum(n_all, n_max), window, 0)
        o = acc_ref[...] / l_ref[...]
        o_ref[0, pl.ds(q0, tq), :] = o[:, :LANES] - lam * o[:, LANES:]
        return carry

    lax.fori_loop(0, nq, q_body, 0)


def _da_attn(q16, k16, v16, nwin, slopes, lams, lam_init):
    b, s, w = q16.shape
    tq, tk = min(DA_TQ, s), min(DA_TK, s)
    spec = pl.BlockSpec((1, s, LANES), lambda i, p, *_: (i, 0, p))
    lspec = pl.BlockSpec((1, DA_HEAD_DIM), lambda i, p, *_: (0, 0))
    return pl.pallas_call(
        functools.partial(_da_attn_kernel, lam_init=lam_init),
        grid_spec=pltpu.PrefetchScalarGridSpec(
            num_scalar_prefetch=1,
            grid=(b, w // LANES),
            in_specs=[pl.BlockSpec(memory_space=pltpu.SMEM), spec, spec, spec,
                      lspec, lspec, lspec, lspec],
            out_specs=spec,
            scratch_shapes=[pltpu.VMEM((tq, tk), F32), pltpu.VMEM((tq, 2 * LANES), F32),
                            pltpu.VMEM((tq, 2 * LANES), F32), pltpu.VMEM((tq, 2 * LANES), F32)],
        ),
        out_shape=jax.ShapeDtypeStruct((b, s, w), F32),
        compiler_params=_cparams(("parallel", "parallel")),
        name="da_attn",
    )(nwin, slopes, q16, k16, v16, *lams)


def _da_window_limits(g_qn, g_kn, slopes, seq):
    tq, tk = min(DA_TQ, seq), min(DA_TK, seq)
    bound = 8.0 * jnp.max(jnp.abs(g_qn)) * jnp.max(jnp.abs(g_kn)) * 1.02 + 0.1
    dist = (2.0 * bound - EXP_ZERO_ARG) / slopes
    nwin = jnp.ceil((dist + tq) / tk) + 1.0
    return jnp.minimum(nwin, 2.0 ** 30).astype(jnp.int32)


def _merge_kernel(sb_ref, da_ref, x_ref, gsb_ref, gda_ref, bd64_ref, bd128_ref, wo_ref, o_ref):
    sb = _chunk_rms(sb_ref[...], bd64_ref[...], gsb_ref[...])
    da = _chunk_rms(da_ref[...], bd128_ref[...], gda_ref[...])
    cat = jnp.concatenate([sb, da], axis=1).astype(BF16)
    o_ref[...] = x_ref[...] + jnp.dot(cat, wo_ref[...], preferred_element_type=F32)


def _merge_out(sb_o, da_o, x, gsb, gda, bd64, bd128, wo16):
    n, d = x.shape
    tm = min(n, 512)
    row = lambda i: (i, 0)
    fix = lambda i: (0, 0)
    return pl.pallas_call(
        _merge_kernel,
        grid=(n // tm,),
        in_specs=[pl.BlockSpec((tm, 512), row), pl.BlockSpec((tm, 512), row), pl.BlockSpec((tm, d), row),
                  pl.BlockSpec((1, 512), fix), pl.BlockSpec((1, 512), fix),
                  pl.BlockSpec((512, 512), fix), pl.BlockSpec((512, 512), fix),
                  pl.BlockSpec(wo16.shape, fix)],
        out_specs=pl.BlockSpec((tm, d), row),
        out_shape=jax.ShapeDtypeStruct((n, d), F32),
        compiler_params=_cparams(("parallel",)),
        name="merge_out",
    )(sb_o, da_o, x, gsb, gda, bd64, bd128, wo16)


def _mem_kv_kernel(x_ref, g_ref, w_ref, gk_ref, bd_ref, k_ref, v_ref):
    m = _rms(x_ref[...], g_ref[...])
    p = jnp.dot(m.astype(BF16), w_ref[...], preferred_element_type=F32)
    k_ref[...] = _chunk_rms(p[:, :512], bd_ref[...], gk_ref[...])
    v_ref[...] = p[:, 512:]


def _mem_kv(mem, g, wkv16, gk, bd128):
    n, d = mem.shape
    tm = min(n, 256)
    row = lambda i: (i, 0)
    fix = lambda i: (0, 0)
    o = jax.ShapeDtypeStruct((n, 512), F32)
    return pl.pallas_call(
        _mem_kv_kernel,
        grid=(n // tm,),
        in_specs=[pl.BlockSpec((tm, d), row), pl.BlockSpec((1, d), fix), pl.BlockSpec(wkv16.shape, fix),
                  pl.BlockSpec((1, 512), fix), pl.BlockSpec((512, 512), fix)],
        out_specs=[pl.BlockSpec((tm, 512), row)] * 2,
        out_shape=[o, o],
        compiler_params=_cparams(("parallel",)),
        name="mem_kv",
    )(mem, g, wkv16, gk, bd128)


def _mem_attn_kernel(x_ref, mk_ref, mv_ref, g_ref, wq_ref, gq_ref, bd_ref, wo_ref, o_ref):
    tm = x_ref.shape[1]
    x = x_ref[0]
    if tm < 8:
        x = jnp.broadcast_to(x[:1], (8, x.shape[1]))
    h = _rms(x, g_ref[...])
    q = jnp.dot(h.astype(BF16), wq_ref[...], preferred_element_type=F32)
    q = (_chunk_rms(q, bd_ref[...], gq_ref[...]) * (LANES ** -0.5)).astype(BF16)
    mk = mk_ref[0].astype(BF16)
    mv = mv_ref[0].astype(BF16)
    outs = []
    for hd in range(4):
        sl = slice(hd * LANES, (hd + 1) * LANES)
        s = lax.dot_general(q[:, sl], mk[:, sl], (((1,), (1,)), ((), ())), preferred_element_type=F32)
        p = jnp.exp(s - jnp.max(s, axis=1, keepdims=True))
        p = p / jnp.sum(p, axis=1, keepdims=True)
        outs.append(jnp.dot(p.astype(BF16), mv[:, sl], preferred_element_type=F32))
    o = jnp.concatenate(outs, axis=1).astype(BF16)
    o_ref[0] = (x + jnp.dot(o, wo_ref[...], preferred_element_type=F32))[:tm]


def _mem_attn(x, mk, mv, g, wq16, gq, bd128, wo16):
    b, t, d = x.shape
    tm = min(t, 512)
    m = mk.shape[1]
    fix = lambda i, j: (0, 0)
    return pl.pallas_call(
        _mem_attn_kernel,
        grid=(b, t // tm),
        in_specs=[pl.BlockSpec((1, tm, d), lambda i, j: (i, j, 0)),
                  pl.BlockSpec((1, m, 512), lambda i, j: (i, 0, 0)),
                  pl.BlockSpec((1, m, 512), lambda i, j: (i, 0, 0)),
                  pl.BlockSpec((1, d), fix), pl.BlockSpec(wq16.shape, fix), pl.BlockSpec((1, 512), fix),
                  pl.BlockSpec((512, 512), fix), pl.BlockSpec(wo16.shape, fix)],
        out_specs=pl.BlockSpec((1, tm, d), lambda i, j: (i, j, 0)),
        out_shape=jax.ShapeDtypeStruct((b, t, d), F32),
        compiler_params=_cparams(("parallel", "parallel")),
        name="mem_attn",
    )(x, mk, mv, g, wq16, gq, bd128, wo16)


def _first_max(vals, lane_f):
    mx = jnp.max(vals, axis=1, keepdims=True)
    idx = jnp.min(jnp.where(vals == mx, lane_f, float(LANES)), axis=1, keepdims=True)
    return mx, idx


def _router_kernel(x_ref, g_ref, whi_ref, wlo_ref, b_ref, route_ref):
    h = _rms(x_ref[...], g_ref[...])
    hi, lo = _split_hi_lo(h)
    lg = (jnp.dot(hi, whi_ref[...], preferred_element_type=F32)
          + jnp.dot(lo, whi_ref[...], preferred_element_type=F32)
          + jnp.dot(hi, wlo_ref[...], preferred_element_type=F32)) + b_ref[...]
    lane = lax.broadcasted_iota(jnp.int32, lg.shape, 1)
    lane_f = lane.astype(F32)
    is_g = lane < N_GROUPS
    gl = jnp.where(is_g, lg, NEG_INF)
    g_max, g_sel = _first_max(gl, lane_f)
    p_g = 1.0 / jnp.sum(jnp.where(is_g, jnp.exp(gl - g_max), 0.0), axis=1, keepdims=True)
    e_lo = N_GROUPS + g_sel * EXPERTS_PER_GROUP
    is_e = jnp.logical_and(lane_f >= e_lo, lane_f < e_lo + EXPERTS_PER_GROUP)
    el = jnp.where(is_e, lg, NEG_INF)
    t1, i1 = _first_max(el, lane_f)
    t2, i2 = _first_max(jnp.where(lane_f == i1, NEG_INF, el), lane_f)
    e2 = jnp.exp(t2 - t1)
    w1 = p_g / (1.0 + e2)
    route_ref[...] = jnp.where(lane == 0, i1 - N_GROUPS,
                               jnp.where(lane == 1, i2 - N_GROUPS,
                                         jnp.where(lane == 2, w1, jnp.where(lane == 3, w1 * e2, 0.0))))


def _router(x, g, whi, wlo, bias):
    n, d = x.shape
    tm = min(n, 512)
    row = lambda i: (i, 0)
    fix = lambda i: (0, 0)
    return pl.pallas_call(
        _router_kernel,
        grid=(n // tm,),
        in_specs=[pl.BlockSpec((tm, d), row), pl.BlockSpec((1, d), fix),
                  pl.BlockSpec((d, LANES), fix), pl.BlockSpec((d, LANES), fix), pl.BlockSpec((1, LANES), fix)],
        out_specs=pl.BlockSpec((tm, LANES), row),
        out_shape=jax.ShapeDtypeStruct((n, LANES), F32),
        compiler_params=_cparams(("parallel",)),
        name="router",
    )(x, g, whi, wlo, bias)


MOE_TM = 256


def _route_plan(route, tm):
    n = route.shape[0]
    n_tiles = (2 * n) // tm + N_EXPERTS
    e_flat = route[:, :2].astype(jnp.int32).reshape(-1)
    order = jnp.argsort(e_flat, stable=True).astype(jnp.int32)
    counts = jnp.sum(e_flat[:, None] == jnp.arange(N_EXPERTS, dtype=jnp.int32)[None, :], axis=0,
                     dtype=jnp.int32)
    tiles_e = (counts + tm - 1) // tm
    tile_end = jnp.cumsum(tiles_e)
    tile_beg = tile_end - tiles_e
    pair_beg = jnp.cumsum(counts) - counts
    j = jnp.arange(n_tiles, dtype=jnp.int32)
    tile_e = jnp.searchsorted(tile_end, j, side="right").astype(jnp.int32)
    te = jnp.minimum(tile_e, N_EXPERTS - 1)
    off = (j - tile_beg[te])[:, None] * tm + jnp.arange(tm, dtype=jnp.int32)[None, :]
    used = (tile_e < N_EXPERTS)[:, None]
    ok = jnp.logical_and(off < counts[te][:, None], used)
    pair = order[jnp.clip(pair_beg[te][:, None] + off, 0, 2 * n - 1)]
    pad = jnp.logical_and(jnp.logical_not(ok), used)
    spare = 2 * n + jnp.cumsum(pad.reshape(-1).astype(jnp.int32)).reshape(pad.shape) - 1
    src = jnp.where(ok, pair // 2, 0)
    dst = jnp.where(ok, (pair % 2) * n + pair // 2, jnp.where(pad, spare, 0))
    idx = jnp.stack([src, dst], axis=1).astype(jnp.int32)
    return tile_e, tile_end[-1:].astype(jnp.int32), idx


def _moe_sparse_kernel(te_ref, nv_ref, g_ref, x_hbm, idx_hbm, wg_ref, wu_ref, wd_ref, y_hbm,
                       idx_smem, xbuf, ybuf, isem, gsem, ssem):
    i = pl.program_id(0)
    n_used = nv_ref[0]
    slot = i % 2
    tm = xbuf.shape[1]

    def idx_copy(tile, s):
        return pltpu.make_async_copy(idx_hbm.at[tile], idx_smem.at[s], isem)

    def gather_row(s, r):
        tok = idx_smem[s, 0, r]
        return pltpu.make_async_copy(x_hbm.at[pl.ds(tok, 1), :], xbuf.at[s, pl.ds(r, 1), :], gsem.at[s])

    def scatter_row(s, r):
        dst = idx_smem[s, 1, r]
        return pltpu.make_async_copy(ybuf.at[s, pl.ds(r, 1), :], y_hbm.at[pl.ds(dst, 1), :], ssem.at[s])

    def gather_all(s):
        return pltpu.make_async_copy(x_hbm.at[pl.ds(0, tm), :], xbuf.at[s], gsem.at[s])

    def scatter_all(s):
        return pltpu.make_async_copy(ybuf.at[s], y_hbm.at[pl.ds(0, tm), :], ssem.at[s])

    def issue(make_row, s):
        def body(r, c):
            make_row(s, r).start()
            return c
        lax.fori_loop(0, tm, body, 0, unroll=8)

    @pl.when(i == 0)
    def _():
        xbuf[1] = jnp.zeros(xbuf.shape[1:], F32)
        spare0 = y_hbm.shape[0] - N_EXPERTS * tm
        fills = [pltpu.make_async_copy(xbuf.at[1], y_hbm.at[pl.ds(spare0 + e * tm, tm), :], ssem.at[1])
                 for e in range(N_EXPERTS)]
        for c in fills:
            c.start()
        for c in fills:
            c.wait()
        idx_copy(0, 0).start()
        idx_copy(0, 0).wait()
        issue(gather_row, 0)

    @pl.when(i < n_used)
    def _():
        more = i + 1 < n_used

        @pl.when(more)
        def _():
            idx_copy(i + 1, 1 - slot).start()

        gather_all(slot).wait()

        @pl.when(more)
        def _():
            idx_copy(i + 1, 1 - slot).wait()
            issue(gather_row, 1 - slot)

        h = _rms(xbuf[slot], g_ref[...]).astype(BF16)
        g = jnp.dot(h, wg_ref[0], preferred_element_type=F32)
        u = jnp.dot(h, wu_ref[0], preferred_element_type=F32)
        a = (g * jax.nn.sigmoid(g)) * u
        y = jnp.dot(a.astype(BF16), wd_ref[0], preferred_element_type=F32)

        @pl.when(i >= 2)
        def _():
            scatter_all(slot).wait()

        ybuf[slot] = y
        issue(scatter_row, slot)

        @pl.when(i == n_used - 1)
        def _():
            scatter_all(slot).wait()

            @pl.when(i >= 1)
            def _():
                scatter_all(1 - slot).wait()


def _moe_sparse(x, route, g, wg16, wu16, wd16):
    n, d = x.shape
    f = wg16.shape[2]
    tm = min(MOE_TM, n)
    tile_e, n_used, idx = _route_plan(route, tm)
    n_tiles = idx.shape[0]
    wmap = lambda i, te, nv: (jnp.minimum(te[i], N_EXPERTS - 1), 0, 0)
    hbm = pl.BlockSpec(memory_space=pl.ANY)
    return pl.pallas_call(
        _moe_sparse_kernel,
        grid_spec=pltpu.PrefetchScalarGridSpec(
            num_scalar_prefetch=2,
            grid=(n_tiles,),
            in_specs=[pl.BlockSpec((1, d), lambda i, te, nv: (0, 0)), hbm, hbm,
                      pl.BlockSpec((1, d, f), wmap), pl.BlockSpec((1, d, f), wmap),
                      pl.BlockSpec((1, f, d), wmap)],
            out_specs=hbm,
            scratch_shapes=[pltpu.SMEM((2, 2, tm), jnp.int32),
                            pltpu.VMEM((2, tm, d), F32), pltpu.VMEM((2, tm, d), F32),
                            pltpu.SemaphoreType.DMA(()), pltpu.SemaphoreType.DMA((2,)),
                            pltpu.SemaphoreType.DMA((2,))],
        ),
        out_shape=jax.ShapeDtypeStruct((2 * n + N_EXPERTS * tm, d), F32),
        compiler_params=_cparams(("arbitrary",)),
        name="moe_sparse",
    )(tile_e, n_used, g, x, idx, wg16, wu16, wd16)


def _moe_combine_kernel(x_ref, y1_ref, y2_ref, route_ref, o_ref):
    r = route_ref[...]
    o_ref[...] = x_ref[...] + r[:, 2:3] * y1_ref[...] + r[:, 3:4] * y2_ref[...]


def _moe_combine(x, y, route):
    n, d = x.shape
    tm = min(n, 512)
    nb = n // tm
    row = lambda i: (i, 0)
    return pl.pallas_call(
        _moe_combine_kernel,
        grid=(nb,),
        in_specs=[pl.BlockSpec((tm, d), row), pl.BlockSpec((tm, d), row),
                  pl.BlockSpec((tm, d), lambda i: (i + nb, 0)), pl.BlockSpec((tm, LANES), row)],
        out_specs=pl.BlockSpec((tm, d), row),
        out_shape=jax.ShapeDtypeStruct((n, d), F32),
        compiler_params=_cparams(("parallel",)),
        name="moe_combine",
    )(x, y, y, route)


PAGES_PER_STEP = 8


def _head_row_mask(width, lanes_per_row):
    lane = lax.broadcasted_iota(jnp.int32, (8, width), 1)
    row = lax.broadcasted_iota(jnp.int32, (8, width), 0)
    return jnp.logical_and(lane >= row * lanes_per_row, lane < (row + 1) * lanes_per_row)


def _sb_decode_kernel(pt_ref, q_ref, u2_ref, kt_hbm, vt_hbm, o_ref, kbuf, vbuf, sem, acc_ref, car_ref,
                      *, layer, n_pages):
    b = pl.program_id(0)
    width = q_ref.shape[2]
    q = jnp.broadcast_to(q_ref[0].astype(F32), (8, width))
    qh = jnp.where(_head_row_mask(width, SB_HEAD_DIM), q, 0.0).astype(BF16)
    acc_ref[...] = jnp.zeros_like(acc_ref)
    car_ref[...] = jnp.zeros_like(car_ref)

    def page_copies(slot):
        page = pt_ref[b * n_pages + slot]
        return (pltpu.make_async_copy(kt_hbm.at[layer, page], kbuf, sem.at[0]),
                pltpu.make_async_copy(vt_hbm.at[layer, page], vbuf, sem.at[1]))

    def cond(st):
        slot, mx = st
        return jnp.logical_and(slot >= 0, mx > EXP_ZERO_ARG)

    def body(st):
        slot, _ = st
        copies = page_copies(slot)
        for c in copies:
            c.start()
        for c in copies:
            c.wait()
        kt = kbuf[...].reshape(width, LANES).astype(BF16)
        vt = vbuf[...].reshape(width, LANES).astype(BF16)
        z = jnp.dot(qh, kt, preferred_element_type=F32)
        sp = _softplus(z)
        hi, lo = _split_hi_lo(-sp)
        r = jnp.dot(jnp.concatenate([hi, lo], axis=1), u2_ref[...], preferred_element_type=F32)
        car = car_ref[...]
        w = jnp.exp((z - sp) + r[:, :LANES] + car)
        acc_ref[...] += lax.dot_general(w.astype(BF16), vt, (((1,), (1,)), ((), ())),
                                        preferred_element_type=F32)
        car = car + r[:, LANES:]
        car_ref[...] = car
        return slot - 1, jnp.max(car)

    lax.while_loop(cond, body, (jnp.int32(n_pages - 1), jnp.float32(0.0)))
    keep = _head_row_mask(width, SB_HEAD_DIM)
    o_ref[0] = jnp.sum(jnp.where(keep, acc_ref[...], 0.0), axis=0, keepdims=True)


def _page_specs(layer, n_pages, page_shape):
    specs = []
    for i in range(PAGES_PER_STEP):
        def imap(b, j, pt, i=i):
            return (layer, pt[b * n_pages + j * PAGES_PER_STEP + i], 0, 0, 0)
        specs.append(pl.BlockSpec((1, 1) + tuple(page_shape), imap))
    return specs


def _sb_decode(q, cache_kt, cache_vt, pt_flat, u2, layer, n_pages):
    b, _, width = q.shape
    page_shape = cache_kt.shape[2:]
    assert page_shape[2] == LANES and page_shape[0] * page_shape[1] == width
    qspec = pl.BlockSpec((1, 1, width), lambda i, pt: (i, 0, 0))
    hbm = pl.BlockSpec(memory_space=pl.ANY)
    return pl.pallas_call(
        functools.partial(_sb_decode_kernel, layer=layer, n_pages=n_pages),
        grid_spec=pltpu.PrefetchScalarGridSpec(
            num_scalar_prefetch=1,
            grid=(b,),
            in_specs=[qspec, pl.BlockSpec(u2.shape, lambda i, pt: (0, 0)), hbm, hbm],
            out_specs=qspec,
            scratch_shapes=[pltpu.VMEM(page_shape, F32), pltpu.VMEM(page_shape, F32),
                            pltpu.SemaphoreType.DMA((2,)),
                            pltpu.VMEM((8, width), F32), pltpu.VMEM((8, LANES), F32)],
        ),
        out_shape=jax.ShapeDtypeStruct((b, 1, width), F32),
        compiler_params=_cparams(("arbitrary",)),
        name="sb_decode",
    )(pt_flat, q, u2, cache_kt, cache_vt)


def _da_rows(vec):
    w = vec.shape[1]
    lane = lax.broadcasted_iota(jnp.int32, (8, w), 1)
    row = lax.broadcasted_iota(jnp.int32, (8, w), 0)
    start = (row % 4) * LANES + (row // 4) * DA_HEAD_DIM
    keep = jnp.logical_and(lane >= start, lane < start + DA_HEAD_DIM)
    return jnp.where(keep, jnp.broadcast_to(vec, (8, w)), 0.0)


def _da_decode_kernel(pt_ref, q_ref, kn_ref, vn_ref, lq1_ref, lk1_ref, lq2_ref, lk2_ref, *refs,
                      lam_init, past_len):
    npg = PAGES_PER_STEP
    k_refs, v_refs = refs[:npg], refs[npg:2 * npg]
    o_ref, m_ref, l_ref, acc_ref = refs[2 * npg:]
    j = pl.program_id(1)
    width = q_ref.shape[2]
    qh = _da_rows(q_ref[0].astype(F32))
    row = lax.broadcasted_iota(jnp.int32, (8, LANES), 0)
    lane = lax.broadcasted_iota(jnp.int32, (8, LANES), 1)
    slope = jnp.exp2(-2.0 * ((row % 4) + 1).astype(F32))

    @pl.when(j == 0)
    def _():
        s_self = jnp.sum(qh * jnp.broadcast_to(kn_ref[0], (8, width)), axis=1, keepdims=True)
        m_ref[...] = jnp.broadcast_to(s_self, m_ref.shape)
        l_ref[...] = jnp.ones_like(l_ref)
        acc_ref[...] = jnp.broadcast_to(vn_ref[0], acc_ref.shape)

    q16 = qh.astype(BF16)
    n_heads = k_refs[0].shape[3]
    nt_dims = (((1,), (1,)), ((), ()))
    s_pages = []
    for i in range(npg):
        s = None
        for hd in range(n_heads):
            k = k_refs[i][0, 0, :, hd, :].astype(BF16)
            d = lax.dot_general(q16[:, hd * LANES:(hd + 1) * LANES], k, nt_dims, preferred_element_type=F32)
            s = d if s is None else s + d
        s_pages.append(s)
    s = jnp.concatenate(s_pages, axis=1)
    k_pos = j * (npg * LANES) + lax.broadcasted_iota(jnp.int32, s.shape, 1)
    s = s - slope[:, :1] * (past_len - k_pos).astype(F32)
    m_prev = m_ref[:, :1]
    m_new = jnp.maximum(m_prev, jnp.max(s, axis=1, keepdims=True))
    p = jnp.exp(s - m_new)
    alpha = jnp.exp(m_prev - m_new)
    l_ref[...] = alpha * l_ref[...] + jnp.sum(p, axis=1, keepdims=True)
    m_ref[...] = jnp.broadcast_to(m_new, m_ref.shape)
    p16 = p.astype(BF16)
    pv = []
    for hd in range(n_heads):
        o = None
        for i in range(npg):
            d = jnp.dot(p16[:, i * LANES:(i + 1) * LANES], v_refs[i][0, 0, :, hd, :].astype(BF16),
                        preferred_element_type=F32)
            o = d if o is None else o + d
        pv.append(o)
    acc_ref[...] = alpha * acc_ref[...] + jnp.concatenate(pv, axis=1)

    @pl.when(j == pl.num_programs(1) - 1)
    def _():
        lam = _lam_value(lq1_ref[...], lk1_ref[...], lq2_ref[...], lk2_ref[...], lam_init)
        lane_w = lax.broadcasted_iota(jnp.int32, (8, width), 1)
        row_w = lax.broadcasted_iota(jnp.int32, (8, width), 0)
        keep = (lane_w // LANES) == (row_w % 4)
        coef = jnp.where(row_w < 4, 1.0, -lam)
        o = acc_ref[...] / l_ref[:, :1] * coef
        o_ref[0] = jnp.sum(jnp.where(keep, o, 0.0), axis=0, keepdims=True)


def _da_decode(q, k_new, v_new, cache_k, cache_v, pt_flat, lams, layer, n_pages, lam_init):
    b, _, width = q.shape
    rows = cache_k.shape[2]
    assert cache_k.shape[4] == LANES and cache_k.shape[3] * LANES == width and rows == LANES
    qspec = pl.BlockSpec((1, 1, width), lambda i, j, pt: (i, 0, 0))
    lspec = pl.BlockSpec((1, DA_HEAD_DIM), lambda i, j, pt: (0, 0))
    pages = _page_specs(layer, n_pages, cache_k.shape[2:])
    return pl.pallas_call(
        functools.partial(_da_decode_kernel, lam_init=lam_init, past_len=n_pages * rows),
        grid_spec=pltpu.PrefetchScalarGridSpec(
            num_scalar_prefetch=1,
            grid=(b, n_pages // PAGES_PER_STEP),
            in_specs=[qspec, qspec, qspec, lspec, lspec, lspec, lspec] + pages + pages,
            out_specs=qspec,
            scratch_shapes=[pltpu.VMEM((8, LANES), F32), pltpu.VMEM((8, LANES), F32),
                            pltpu.VMEM((8, width), F32)],
        ),
        out_shape=jax.ShapeDtypeStruct((b, 1, width), F32),
        compiler_params=_cparams(("parallel", "arbitrary")),
        name="da_decode",
    )(pt_flat, q, k_new, v_new, *lams, *([cache_k] * PAGES_PER_STEP), *([cache_v] * PAGES_PER_STEP))


def _tile_gain(g, reps):
    return jnp.tile(g.astype(F32), reps)[None, :]


def kernel(x_prompt, x_sample, mem_prompt, cache_sb_k, cache_sb_v, cache_da_k, cache_da_v, cache_mem_k, cache_mem_v, page_table, g_attn, w_in, g_da_qn, g_da_kn, lam_q1, lam_k1, lam_q2, lam_k2, g_sb_out, g_da_out, w_o, g_mem_x, g_mem_src, w_mq, w_mk, w_mv, w_mo, g_mem_qn, g_mem_kn, g_ffn, w_rg, b_rg, w_re, b_re, w_gate, w_up, w_down):
    depth = w_in.shape[0]
    bp, sp, d = x_prompt.shape
    bs, ss, _ = x_sample.shape
    assert ss == 1, "the sample group decodes one new token per sequence"
    n_pages, page = page_table.shape[1], cache_sb_k.shape[2]
    n_mem = mem_prompt.shape[1]

    bd64 = _block_diag_mean(512, 64)
    bd128 = _block_diag_mean(512, 128)
    u2 = _cumsum_matrix()
    slopes = jnp.exp2(-8.0 * jnp.arange(1, 5, dtype=F32) / 4)
    pt_flat = page_table.reshape(-1).astype(jnp.int32)
    csbk = jnp.transpose(cache_sb_k, (0, 1, 3, 4, 2))
    csbv = jnp.transpose(cache_sb_v, (0, 1, 3, 4, 2))
    cdak, cdav = cache_da_k, cache_da_v

    xp = x_prompt.reshape(bp * sp, d)
    xs = x_sample.reshape(bs, d)
    mem = mem_prompt.reshape(bp * n_mem, d)
    outs = [[] for _ in range(10)]

    for l in range(depth):
        lam_init = 0.8 - 0.6 * math.exp(-0.3 * l)
        row = lambda v: v[l].astype(F32)[None, :]
        w_in16 = w_in[l].astype(BF16)
        w_o16 = w_o[l].astype(BF16)
        w_mq16, w_mo16 = w_mq[l].astype(BF16), w_mo[l].astype(BF16)
        w_mkv16 = jnp.concatenate([w_mk[l], w_mv[l]], axis=1).astype(BF16)
        wg16, wu16, wd16 = w_gate[l].astype(BF16), w_up[l].astype(BF16), w_down[l].astype(BF16)
        w_r = jnp.zeros((d, LANES), F32).at[:, :N_GROUPS].set(w_rg[l]).at[:, N_GROUPS:N_GROUPS + N_EXPERTS].set(w_re[l])
        w_r_hi = w_r.astype(BF16)
        w_r_lo = (w_r - w_r_hi.astype(F32)).astype(BF16)
        b_r = jnp.zeros((1, LANES), F32).at[0, :N_GROUPS].set(b_rg[l]).at[0, N_GROUPS:N_GROUPS + N_EXPERTS].set(b_re[l])
        gq, gk = _tile_gain(g_da_qn[l], 8), _tile_gain(g_da_kn[l], 8)
        gsb = _tile_gain(g_sb_out[l], 8)
        gda = _tile_gain(g_da_out[l], 4) * (1.0 - lam_init)
        gmq, gmk = _tile_gain(g_mem_qn[l], 4), _tile_gain(g_mem_kn[l], 4)
        lams = (row(lam_q1), row(lam_k1), row(lam_q2), row(lam_k2))
        nwin = _da_window_limits(g_da_qn[l], g_da_kn[l], slopes, sp)

        def ffn(x):
            route = _router(x, row(g_ffn), w_r_hi, w_r_lo, b_r)
            y = _moe_sparse(x, route, row(g_ffn), wg16, wu16, wd16)
            return _moe_combine(x, y, route)

        (sbk_s, sbv_s, dak_s, dav_s, sbq16, _, _, daq16, _, _) = _in_proj(xs, row(g_attn), w_in16, gq, gk, bd64)
        as1 = lambda t: t.reshape(bs, 1, 512)
        sb_o = _sb_decode(as1(sbq16), csbk, csbv, pt_flat, u2, l, n_pages)
        da_o = _da_decode(as1(daq16), as1(dak_s), as1(dav_s), cdak, cdav, pt_flat, lams, l, n_pages, lam_init)
        xs = _merge_out(sb_o.reshape(bs, 512), da_o.reshape(bs, 512), xs, gsb, gda, bd64, bd128, w_o16)
        xs = _mem_attn(xs.reshape(bs, 1, d), cache_mem_k[l].reshape(bs, n_mem, 512),
                       cache_mem_v[l].reshape(bs, n_mem, 512),
                       row(g_mem_x), w_mq16, gmq, bd128, w_mo16).reshape(bs, d)
        xs = ffn(xs)
        for dst, val in zip(outs[6:], (sbk_s, sbv_s, dak_s, dav_s)):
            dst.append(val)

        (sbk, sbv, dak, dav, sbq16, sbk16, sbv16, daq16, dak16, dav16) = _in_proj(
            xp, row(g_attn), w_in16, gq, gk, bd64)
        as3 = lambda t: t.reshape(bp, sp, 512)
        sb_o = _sb_attn(as3(sbq16), as3(sbk16), as3(sbv16), u2)
        da_o = _da_attn(as3(daq16), as3(dak16), as3(dav16), nwin, slopes, lams, lam_init)
        xp = _merge_out(sb_o.reshape(-1, 512), da_o.reshape(-1, 512), xp, gsb, gda, bd64, bd128, w_o16)
        mk, mv = _mem_kv(mem, row(g_mem_src), w_mkv16, gmk, bd128)
        xp = _mem_attn(xp.reshape(bp, sp, d), mk.reshape(bp, n_mem, 512), mv.reshape(bp, n_mem, 512),
                       row(g_mem_x), w_mq16, gmq, bd128, w_mo16).reshape(-1, d)
        xp = ffn(xp)
        for dst, val in zip(outs[:6], (sbk, sbv, dak, dav, mk, mv)):
            dst.append(val)

    st =lambda lst, shape: jnp.stack(lst, axis=0).reshape(shape)
    return (xp.reshape(bp, sp, d), xs.reshape(bs, 1, d),
            st(outs[0], (depth, bp, sp, 8, 64)), st(outs[1], (depth, bp, sp, 8, 64)),
            st(outs[2], (depth, bp, sp, 4, 128)), st(outs[3], (depth, bp, sp, 4, 128)),
            st(outs[4], (depth, bp, n_mem, 4, 128)), st(outs[5], (depth, bp, n_mem, 4, 128)),
            st(outs[6], (depth, bs, 1, 8, 64)), st(outs[7], (depth, bs, 1, 8, 64)),
            st(outs[8], (depth, bs, 1, 4, 128)), st(outs[9], (depth, bs, 1, 4, 128)))
```

```python
import functools
import math

import jax
import jax.numpy as jnp
from jax import lax
from jax.experimental import pallas as pl
from jax.experimental.pallas import tpu as pltpu

F32 = jnp.float32
BF16 = jnp.bfloat16
RMS_EPS = 1e-6
NEG_INF = float("-inf")

LANES = 128
SB_HEAD_DIM = 64
DA_HEAD_DIM = 64
N_GROUPS = 4
EXPERTS_PER_GROUP = 8
N_EXPERTS = N_GROUPS * EXPERTS_PER_GROUP
EXP_ZERO_ARG = -104.0
VMEM_LIMIT = 56 * 1024 * 1024


def _cparams(sem):
    return pltpu.CompilerParams(dimension_semantics=sem, vmem_limit_bytes=VMEM_LIMIT)


def _block_diag_mean(width, chunk):
    i = jnp.arange(width) // chunk
    return jnp.where(i[:, None] == i[None, :], 1.0 / chunk, 0.0).astype(BF16)


def _cumsum_matrix():
    j = jnp.arange(2 * LANES) % LANES
    s = jnp.arange(2 * LANES)
    return jnp.where((s[None, :] >= LANES) | (j[:, None] > s[None, :]), 1.0, 0.0).astype(BF16)


def _rms(x, g):
    return x * lax.rsqrt(jnp.mean(x * x, axis=-1, keepdims=True) + RMS_EPS) * g


def _chunk_rms(t, bd, g):
    ms = jnp.dot((t * t).astype(BF16), bd, preferred_element_type=F32)
    return t * lax.rsqrt(ms + RMS_EPS) * g


def _split_hi_lo(x):
    hi = x.astype(BF16)
    lo = (x - hi.astype(F32)).astype(BF16)
    return hi, lo


def _softplus(z):
    return jnp.maximum(z, 0.0) + jnp.log(1.0 + jnp.exp(-jnp.abs(z)))


def _in_proj_kernel(x_ref, g_ref, w_ref, gq_ref, gk_ref, bd_ref,
                    sbk_ref, sbv_ref, dak_ref, dav_ref,
                    sbq16_ref, sbk16_ref, sbv16_ref, daq16_ref, dak16_ref, dav16_ref):
    h = _rms(x_ref[...], g_ref[...])
    p = jnp.dot(h.astype(BF16), w_ref[...], preferred_element_type=F32)
    sb_q, sb_k, sb_v = p[:, 0:512], p[:, 512:1024], p[:, 1024:1536]
    da_q = _chunk_rms(p[:, 1536:2048], bd_ref[...], gq_ref[...])
    da_k = _chunk_rms(p[:, 2048:2560], bd_ref[...], gk_ref[...])
    da_v = p[:, 2560:3072]
    sbk_ref[...] = sb_k
    sbv_ref[...] = sb_v
    dak_ref[...] = da_k
    dav_ref[...] = da_v
    sbq16_ref[...] = (sb_q * 0.125).astype(BF16)
    sbk16_ref[...] = sb_k.astype(BF16)
    sbv16_ref[...] = sb_v.astype(BF16)
    daq16_ref[...] = (da_q * 0.125).astype(BF16)
    dak16_ref[...] = da_k.astype(BF16)
    dav16_ref[...] = da_v.astype(BF16)


def _in_proj(x, g, w16, gq, gk, bd64):
    n, d = x.shape
    tm = min(n, 256)
    row = lambda i: (i, 0)
    fix = lambda i: (0, 0)
    o32 = jax.ShapeDtypeStruct((n, 512), F32)
    o16 = jax.ShapeDtypeStruct((n, 512), BF16)
    ospec = pl.BlockSpec((tm, 512), row)
    return pl.pallas_call(
        _in_proj_kernel,
        grid=(n // tm,),
        in_specs=[pl.BlockSpec((tm, d), row), pl.BlockSpec((1, d), fix),
                  pl.BlockSpec(w16.shape, fix), pl.BlockSpec((1, 512), fix),
                  pl.BlockSpec((1, 512), fix), pl.BlockSpec((512, 512), fix)],
        out_specs=[ospec] * 10,
        out_shape=[o32] * 4 + [o16] * 6,
        compiler_params=_cparams(("parallel",)),
        name="in_proj",
    )(x, g, w16, gq, gk, bd64)


def _pair_rows(t, in_a):
    zero = jnp.zeros_like(t)
    return jnp.concatenate([jnp.where(in_a, t, zero), jnp.where(in_a, zero, t)], axis=0)


def _sb_block(q, k, v, u2, valid2, car, in_a):
    z = lax.dot_general(q, _pair_rows(k, in_a), (((1,), (1,)), ((), ())),
                        preferred_element_type=F32)
    sp = _softplus(z)
    l1m = -sp if valid2 is None else jnp.where(valid2, -sp, 0.0)
    hi, lo = _split_hi_lo(l1m)
    r_a = jnp.dot(jnp.concatenate([hi[:, :LANES], lo[:, :LANES]], axis=1), u2,
                  preferred_element_type=F32)
    r_b = jnp.dot(jnp.concatenate([hi[:, LANES:], lo[:, LANES:]], axis=1), u2,
                  preferred_element_type=F32)
    after = jnp.concatenate([r_a[:, :LANES], r_b[:, :LANES]], axis=1)
    total = jnp.concatenate([r_a[:, LANES:], r_b[:, LANES:]], axis=1)
    w = jnp.exp((z - sp) + after + car)
    if valid2 is not None:
        w = jnp.where(valid2, w, 0.0)
    pv = jnp.dot(w.astype(BF16), _pair_rows(v, in_a), preferred_element_type=F32)
    return pv, total


SB_Q_CHUNK = 1024


def _sb_attn_kernel(q_ref, k_ref, v_ref, u2_ref, o_ref, acc_ref, car_ref):
    t = LANES
    n_pairs = q_ref.shape[2] // LANES
    nq = q_ref.shape[1] // t
    chunk = pl.program_id(1)
    lane = lax.broadcasted_iota(jnp.int32, (t, LANES), 1)
    row = lax.broadcasted_iota(jnp.int32, (t, LANES), 0)
    in_a = lane < SB_HEAD_DIM
    col_minus_row = lane - row

    def q_body(qq, carry):
        qi = chunk * nq + qq
        qs = pl.ds(pl.multiple_of(qq * t, t), t)
        acc_ref[...] = jnp.zeros_like(acc_ref)
        car_ref[...] = jnp.zeros_like(car_ref)

        def cond(st):
            kj, mx = st
            return jnp.logical_and(kj >= 0, mx > EXP_ZERO_ARG)

        def body(st):
            kj, _ = st
            ks = pl.ds(pl.multiple_of(kj * t, t), t)
            valid = col_minus_row < (qi - kj) * t
            valid2 = jnp.concatenate([valid, valid], axis=1)
            mx = None
            for p in range(n_pairs):
                ln = slice(p * LANES, (p + 1) * LANES)
                car = car_ref[p]
                pv, total = _sb_block(q_ref[0, qs, ln], k_ref[0, ks, ln], v_ref[0, ks, ln],
                                      u2_ref[...], valid2, car, in_a)
                acc_ref[p] += pv
                car = car + total
                car_ref[p] = car
                mx = jnp.max(car) if mx is None else jnp.maximum(mx, jnp.max(car))
            return kj - 1, mx

        lax.while_loop(cond, body, (qi, jnp.float32(0.0)))
        o_ref[0, qs, :] = jnp.concatenate([acc_ref[p] for p in range(n_pairs)], axis=1)
        return carry

    lax.fori_loop(0, nq, q_body, 0)


def _sb_attn(q16, k16, v16, u2):
    b, s, w = q16.shape
    qc = min(s, SB_Q_CHUNK)
    n_pairs = w // LANES
    qspec = pl.BlockSpec((1, qc, w), lambda i, c: (i, c, 0))
    kspec = pl.BlockSpec((1, s, w), lambda i, c: (i, 0, 0))
    return pl.pallas_call(
        _sb_attn_kernel,
        grid=(b, s // qc),
        in_specs=[qspec, kspec, kspec, pl.BlockSpec(u2.shape, lambda i, c: (0, 0))],
        out_specs=qspec,
        out_shape=jax.ShapeDtypeStruct((b, s, w), F32),
        scratch_shapes=[pltpu.VMEM((n_pairs, LANES, LANES), F32),
                        pltpu.VMEM((n_pairs, LANES, 2 * LANES), F32)],
        compiler_params=_cparams(("parallel", "arbitrary")),
        name="sb_attn",
    )(q16, k16, v16, u2)


def _lam_value(lq1, lk1, lq2, lk2, lam_init):
    f = lambda a, b: jnp.exp(jnp.sum(a * b, axis=-1, keepdims=True))
    return f(lq1, lk1) - f(lq2, lk2) + lam_init


FIXED_SHIFT_MAX = 60.0
DA_TQ = 256
DA_TK = 512


def _da_attn_kernel(nwin_ref, slope_ref, q_ref, k_ref, v_ref, lq1_ref, lk1_ref, lq2_ref, lk2_ref,
                    o_ref, bias_ref, m_ref, l_ref, acc_ref, *, lam_init):
    tq = min(DA_TQ, q_ref.shape[1])
    tk = min(DA_TK, q_ref.shape[1])
    nq = q_ref.shape[1] // tq
    h = pl.program_id(1)
    slope = slope_ref[h]
    n_max = nwin_ref[h]
    bound = slope_ref[slope_ref.shape[0] - 1]
    bounded = nwin_ref[nwin_ref.shape[0] - 1] == 1
    lane = lax.broadcasted_iota(jnp.int32, (tq, tk), 1)
    row = lax.broadcasted_iota(jnp.int32, (tq, tk), 0)
    col_minus_row = lane - row
    bias_ref[...] = slope * col_minus_row.astype(F32)
    in_a = lax.broadcasted_iota(jnp.int32, (tq, LANES), 1) < DA_HEAD_DIM
    lam = _lam_value(lq1_ref[...], lk1_ref[...], lq2_ref[...], lk2_ref[...], lam_init)
    nt_dims = (((1,), (1,)), ((), ()))

    def q_body(qi, carry):
        q0 = pl.multiple_of(qi * tq, tq)
        end = q0 + tq
        q = q_ref[0, pl.ds(q0, tq), :]
        zero = jnp.zeros_like(q)
        q_maps = (jnp.where(in_a, q, zero), jnp.where(in_a, zero, q))
        m_ref[...] = jnp.full_like(m_ref, NEG_INF)
        l_ref[...] = jnp.zeros_like(l_ref)
        acc_ref[...] = jnp.zeros_like(acc_ref)

        def window(t, c):
            hi = end - t * tk
            start = pl.multiple_of(jnp.maximum(hi - tk, 0), tq)
            k = k_ref[0, pl.ds(start, tk), :]
            v = v_ref[0, pl.ds(start, tk), :]
            shift = slope * (q0 - start).astype(F32)

            def update(valid, running_max):
                for i in range(2):
                    ln = slice(i * LANES, (i + 1) * LANES)
                    s = lax.dot_general(q_maps[i], k, nt_dims, preferred_element_type=F32) + bias_ref[...]
                    if valid is not None:
                        s = jnp.where(valid, s, NEG_INF)
                    if running_max:
                        m_prev = m_ref[:, ln]
                        m_new = jnp.maximum(m_prev, jnp.max(s, axis=1, keepdims=True) - shift)
                        p = jnp.exp(s - (m_new[:, :1] + shift))
                        alpha = jnp.exp(m_prev - m_new)
                        l_ref[:, ln] = alpha * l_ref[:, ln] + jnp.sum(p, axis=1, keepdims=True)
                        m_ref[:, ln] = m_new
                        acc_ref[:, ln] = alpha * acc_ref[:, ln] + jnp.dot(
                            p.astype(BF16), v, preferred_element_type=F32)
                    else:
                        p = jnp.exp(s - (bound + shift))
                        l_ref[:, ln] += jnp.sum(p, axis=1, keepdims=True)
                        acc_ref[:, ln] += jnp.dot(p.astype(BF16), v, preferred_element_type=F32)

            def mask():
                return jnp.logical_and(col_minus_row <= q0 - start, lane < hi - start)

            def fixed_shift():
                lax.cond(jnp.logical_or(t == 0, hi < tk),
                         lambda: update(mask(), False), lambda: update(None, False))

            lax.cond(bounded, fixed_shift, lambda: update(mask(), True))
            return c

        n_all = lax.div(end + (tk - 1), tk)
        lax.fori_loop(0, jnp.minimum(n_all, n_max), window, 0)
        o = acc_ref[...] / l_ref[...]
        o_ref[0, pl.ds(q0, tq), :] = o[:, :LANES] - lam * o[:, LANES:]
        return carry

    lax.fori_loop(0, nq, q_body, 0)


def _da_attn(q16, k16, v16, nwin, slopes, lams, lam_init):
    b, s, w = q16.shape
    tq, tk = min(DA_TQ, s), min(DA_TK, s)
    spec = pl.BlockSpec((1, s, LANES), lambda i, p, *_: (i, 0, p))
    lspec = pl.BlockSpec((1, DA_HEAD_DIM), lambda i, p, *_: (0, 0))
    return pl.pallas_call(
        functools.partial(_da_attn_kernel, lam_init=lam_init),
        grid_spec=pltpu.PrefetchScalarGridSpec(
            num_scalar_prefetch=1,
            grid=(b, w // LANES),
            in_specs=[pl.BlockSpec(memory_space=pltpu.SMEM), spec, spec, spec,
                      lspec, lspec, lspec, lspec],
            out_specs=spec,
            scratch_shapes=[pltpu.VMEM((tq, tk), F32), pltpu.VMEM((tq, 2 * LANES), F32),
                            pltpu.VMEM((tq, 2 * LANES), F32), pltpu.VMEM((tq, 2 * LANES), F32)],
        ),
        out_shape=jax.ShapeDtypeStruct((b, s, w), F32),
        compiler_params=_cparams(("parallel", "parallel")),
        name="da_attn",
    )(nwin, slopes, q16, k16, v16, *lams)


def _da_window_limits(g_qn, g_kn, slopes, seq):
    tq, tk = min(DA_TQ, seq), min(DA_TK, seq)
    bound = 8.0 * jnp.max(jnp.abs(g_qn)) * jnp.max(jnp.abs(g_kn)) * 1.02 + 0.1
    dist = (2.0 * bound - EXP_ZERO_ARG) / slopes
    nwin = jnp.minimum(jnp.ceil((dist + tq) / tk) + 1.0, 2.0 ** 30).astype(jnp.int32)
    fixed_ok = (2.0 * bound <= FIXED_SHIFT_MAX).astype(jnp.int32)
    return (jnp.concatenate([nwin, fixed_ok[None]]),
            jnp.concatenate([slopes, bound[None].astype(F32)]))


def _merge_kernel(sb_ref, da_ref, x_ref, gsb_ref, gda_ref, bd64_ref, bd128_ref, wo_ref, o_ref):
    sb = _chunk_rms(sb_ref[...], bd64_ref[...], gsb_ref[...])
    da = _chunk_rms(da_ref[...], bd128_ref[...], gda_ref[...])
    cat = jnp.concatenate([sb, da], axis=1).astype(BF16)
    o_ref[...] = x_ref[...] + jnp.dot(cat, wo_ref[...], preferred_element_type=F32)


def _merge_out(sb_o, da_o, x, gsb, gda, bd64, bd128, wo16):
    n, d = x.shape
    tm = min(n, 512)
    row = lambda i: (i, 0)
    fix = lambda i: (0, 0)
    return pl.pallas_call(
        _merge_kernel,
        grid=(n // tm,),
        in_specs=[pl.BlockSpec((tm, 512), row), pl.BlockSpec((tm, 512), row), pl.BlockSpec((tm, d), row),
                  pl.BlockSpec((1, 512), fix), pl.BlockSpec((1, 512), fix),
                  pl.BlockSpec((512, 512), fix), pl.BlockSpec((512, 512), fix),
                  pl.BlockSpec(wo16.shape, fix)],
        out_specs=pl.BlockSpec((tm, d), row),
        out_shape=jax.ShapeDtypeStruct((n, d), F32),
        compiler_params=_cparams(("parallel",)),
        name="merge_out",
    )(sb_o, da_o, x, gsb, gda, bd64, bd128, wo16)


def _mem_kv_kernel(x_ref, g_ref, w_ref, gk_ref, bd_ref, k_ref, v_ref):
    m = _rms(x_ref[...], g_ref[...])
    p = jnp.dot(m.astype(BF16), w_ref[...], preferred_element_type=F32)
    k_ref[...] = _chunk_rms(p[:, :512], bd_ref[...], gk_ref[...])
    v_ref[...] = p[:, 512:]


def _mem_kv(mem, g, wkv16, gk, bd128):
    n, d = mem.shape
    tm = min(n, 256)
    row = lambda i: (i, 0)
    fix = lambda i: (0, 0)
    o = jax.ShapeDtypeStruct((n, 512), F32)
    return pl.pallas_call(
        _mem_kv_kernel,
        grid=(n // tm,),
        in_specs=[pl.BlockSpec((tm, d), row), pl.BlockSpec((1, d), fix), pl.BlockSpec(wkv16.shape, fix),
                  pl.BlockSpec((1, 512), fix), pl.BlockSpec((512, 512), fix)],
        out_specs=[pl.BlockSpec((tm, 512), row)] * 2,
        out_shape=[o, o],
        compiler_params=_cparams(("parallel",)),
        name="mem_kv",
    )(mem, g, wkv16, gk, bd128)


def _mem_attn_kernel(x_ref, mk_ref, mv_ref, g_ref, wq_ref, gq_ref, bd_ref, wo_ref, o_ref):
    tm = x_ref.shape[1]
    x = x_ref[0]
    if tm < 8:
        x = jnp.broadcast_to(x[:1], (8, x.shape[1]))
    h = _rms(x, g_ref[...])
    q = jnp.dot(h.astype(BF16), wq_ref[...], preferred_element_type=F32)
    q = (_chunk_rms(q, bd_ref[...], gq_ref[...]) * (LANES ** -0.5)).astype(BF16)
    mk = mk_ref[0].astype(BF16)
    mv = mv_ref[0].astype(BF16)
    outs = []
    for hd in range(4):
        sl = slice(hd * LANES, (hd + 1) * LANES)
        s = lax.dot_general(q[:, sl], mk[:, sl], (((1,), (1,)), ((), ())), preferred_element_type=F32)
        p = jnp.exp(s - jnp.max(s, axis=1, keepdims=True))
        p = p / jnp.sum(p, axis=1, keepdims=True)
        outs.append(jnp.dot(p.astype(BF16), mv[:, sl], preferred_element_type=F32))
    o = jnp.concatenate(outs, axis=1).astype(BF16)
    o_ref[0] = (x + jnp.dot(o, wo_ref[...], preferred_element_type=F32))[:tm]


def _mem_attn(x, mk, mv, g, wq16, gq, bd128, wo16):
    b, t, d = x.shape
    tm = min(t, 512)
    m = mk.shape[1]
    fix = lambda i, j: (0, 0)
    return pl.pallas_call(
        _mem_attn_kernel,
        grid=(b, t // tm),
        in_specs=[pl.BlockSpec((1, tm, d), lambda i, j: (i, j, 0)),
                  pl.BlockSpec((1, m, 512), lambda i, j: (i, 0, 0)),
                  pl.BlockSpec((1, m, 512), lambda i, j: (i, 0, 0)),
                  pl.BlockSpec((1, d), fix), pl.BlockSpec(wq16.shape, fix), pl.BlockSpec((1, 512), fix),
                  pl.BlockSpec((512, 512), fix), pl.BlockSpec(wo16.shape, fix)],
        out_specs=pl.BlockSpec((1, tm, d), lambda i, j: (i, j, 0)),
        out_shape=jax.ShapeDtypeStruct((b, t, d), F32),
        compiler_params=_cparams(("parallel", "parallel")),
        name="mem_attn",
    )(x, mk, mv, g, wq16, gq, bd128, wo16)


def _first_max(vals, lane_f):
    mx = jnp.max(vals, axis=1, keepdims=True)
    idx = jnp.min(jnp.where(vals == mx, lane_f, float(LANES)), axis=1, keepdims=True)
    return mx, idx


def _router_kernel(x_ref, g_ref, whi_ref, wlo_ref, b_ref, route_ref):
    h = _rms(x_ref[...], g_ref[...])
    hi, lo = _split_hi_lo(h)
    lg = (jnp.dot(hi, whi_ref[...], preferred_element_type=F32)
          + jnp.dot(lo, whi_ref[...], preferred_element_type=F32)
          + jnp.dot(hi, wlo_ref[...], preferred_element_type=F32)) + b_ref[...]
    lane = lax.broadcasted_iota(jnp.int32, lg.shape, 1)
    lane_f = lane.astype(F32)
    is_g = lane < N_GROUPS
    gl = jnp.where(is_g, lg, NEG_INF)
    g_max, g_sel = _first_max(gl, lane_f)
    p_g = 1.0 / jnp.sum(jnp.where(is_g, jnp.exp(gl - g_max), 0.0), axis=1, keepdims=True)
    e_lo = N_GROUPS + g_sel * EXPERTS_PER_GROUP
    is_e = jnp.logical_and(lane_f >= e_lo, lane_f < e_lo + EXPERTS_PER_GROUP)
    el = jnp.where(is_e, lg, NEG_INF)
    t1, i1 = _first_max(el, lane_f)
    t2, i2 = _first_max(jnp.where(lane_f == i1, NEG_INF, el), lane_f)
    e2 = jnp.exp(t2 - t1)
    w1 = p_g / (1.0 + e2)
    route_ref[...] = jnp.where(lane == 0, i1 - N_GROUPS,
                               jnp.where(lane == 1, i2 - N_GROUPS,
                                         jnp.where(lane == 2, w1, jnp.where(lane == 3, w1 * e2, 0.0))))


def _router(x, g, whi, wlo, bias):
    n, d = x.shape
    tm = min(n, 512)
    row = lambda i: (i, 0)
    fix = lambda i: (0, 0)
    return pl.pallas_call(
        _router_kernel,
        grid=(n // tm,),
        in_specs=[pl.BlockSpec((tm, d), row), pl.BlockSpec((1, d), fix),
                  pl.BlockSpec((d, LANES), fix), pl.BlockSpec((d, LANES), fix), pl.BlockSpec((1, LANES), fix)],
        out_specs=pl.BlockSpec((tm, LANES), row),
        out_shape=jax.ShapeDtypeStruct((n, LANES), F32),
        compiler_params=_cparams(("parallel",)),
        name="router",
    )(x, g, whi, wlo, bias)


MOE_TM = 256


def _route_plan(route, tm):
    n = route.shape[0]
    n_tiles = (2 * n) // tm + N_EXPERTS
    e_flat = route[:, :2].astype(jnp.int32).reshape(-1)
    order = jnp.argsort(e_flat, stable=True).astype(jnp.int32)
    counts = jnp.sum(e_flat[:, None] == jnp.arange(N_EXPERTS, dtype=jnp.int32)[None, :], axis=0,
                     dtype=jnp.int32)
    tiles_e = (counts + tm - 1) // tm
    tile_end = jnp.cumsum(tiles_e)
    tile_beg = tile_end - tiles_e
    pair_beg = jnp.cumsum(counts) - counts
    j = jnp.arange(n_tiles, dtype=jnp.int32)
    tile_e = jnp.sum(j[:, None] >= tile_end[None, :], axis=1, dtype=jnp.int32)
    te = jnp.minimum(tile_e, N_EXPERTS - 1)
    off = (j - tile_beg[te])[:, None] * tm + jnp.arange(tm, dtype=jnp.int32)[None, :]
    used = (tile_e < N_EXPERTS)[:, None]
    ok = jnp.logical_and(off < counts[te][:, None], used)
    pair = order[jnp.clip(pair_beg[te][:, None] + off, 0, 2 * n - 1)]
    pad = jnp.logical_and(jnp.logical_not(ok), used)
    spare = 2 * n + jnp.cumsum(pad.reshape(-1).astype(jnp.int32)).reshape(pad.shape) - 1
    src = jnp.where(ok, pair // 2, 0)
    dst = jnp.where(ok, (pair % 2) * n + pair // 2, jnp.where(pad, spare, 0))
    idx = jnp.stack([src, dst], axis=1).astype(jnp.int32)
    return tile_e, tile_end[-1:].astype(jnp.int32), idx


def _moe_sparse_kernel(te_ref, nv_ref, g_ref, x_hbm, idx_hbm, wg_ref, wu_ref, wd_ref, y_hbm,
                       idx_smem, xbuf, ybuf, isem, gsem, ssem):
    i = pl.program_id(0)
    n_used = nv_ref[0]
    slot = i % 2
    tm = xbuf.shape[1]

    def idx_copy(tile, s):
        return pltpu.make_async_copy(idx_hbm.at[tile], idx_smem.at[s], isem)

    def gather_row(s, r):
        tok = idx_smem[s, 0, r]
        return pltpu.make_async_copy(x_hbm.at[pl.ds(tok, 1), :], xbuf.at[s, pl.ds(r, 1), :], gsem.at[s])

    def scatter_row(s, r):
        dst = idx_smem[s, 1, r]
        return pltpu.make_async_copy(ybuf.at[s, pl.ds(r, 1), :], y_hbm.at[pl.ds(dst, 1), :], ssem.at[s])

    def gather_all(s):
        return pltpu.make_async_copy(x_hbm.at[pl.ds(0, tm), :], xbuf.at[s], gsem.at[s])

    def scatter_all(s):
        return pltpu.make_async_copy(ybuf.at[s], y_hbm.at[pl.ds(0, tm), :], ssem.at[s])

    def issue(make_row, s):
        def body(r, c):
            make_row(s, r).start()
            return c
        lax.fori_loop(0, tm, body, 0, unroll=8)

    @pl.when(i == 0)
    def _():
        xbuf[1] = jnp.zeros(xbuf.shape[1:], F32)
        spare0 = y_hbm.shape[0] - N_EXPERTS * tm
        fills = [pltpu.make_async_copy(xbuf.at[1], y_hbm.at[pl.ds(spare0 + e * tm, tm), :], ssem.at[1])
                 for e in range(N_EXPERTS)]
        for c in fills:
            c.start()
        for c in fills:
            c.wait()
        idx_copy(0, 0).start()
        idx_copy(0, 0).wait()
        issue(gather_row, 0)

    @pl.when(i < n_used)
    def _():
        more = i + 1 < n_used

        @pl.when(more)
        def _():
            idx_copy(i + 1, 1 - slot).start()

        gather_all(slot).wait()

        @pl.when(more)
        def _():
            idx_copy(i + 1, 1 - slot).wait()
            issue(gather_row, 1 - slot)

        h = _rms(xbuf[slot], g_ref[...]).astype(BF16)
        g = jnp.dot(h, wg_ref[0], preferred_element_type=F32)
        u = jnp.dot(h, wu_ref[0], preferred_element_type=F32)
        a = (g * jax.nn.sigmoid(g)) * u
        y = jnp.dot(a.astype(BF16), wd_ref[0], preferred_element_type=F32)

        @pl.when(i >= 2)
        def _():
            scatter_all(slot).wait()

        ybuf[slot] = y
        issue(scatter_row, slot)

        @pl.when(i == n_used - 1)
        def _():
            scatter_all(slot).wait()

            @pl.when(i >= 1)
            def _():
                scatter_all(1 - slot).wait()


def _moe_sparse(x, route, g, wg16, wu16, wd16):
    n, d = x.shape
    f = wg16.shape[2]
    tm = min(MOE_TM, n)
    tile_e, n_used, idx = _route_plan(route, tm)
    n_tiles = idx.shape[0]
    wmap = lambda i, te, nv: (jnp.minimum(te[i], N_EXPERTS - 1), 0, 0)
    hbm = pl.BlockSpec(memory_space=pl.ANY)
    return pl.pallas_call(
        _moe_sparse_kernel,
        grid_spec=pltpu.PrefetchScalarGridSpec(
            num_scalar_prefetch=2,
            grid=(n_tiles,),
            in_specs=[pl.BlockSpec((1, d), lambda i, te, nv: (0, 0)), hbm, hbm,
                      pl.BlockSpec((1, d, f), wmap), pl.BlockSpec((1, d, f), wmap),
                      pl.BlockSpec((1, f, d), wmap)],
            out_specs=hbm,
            scratch_shapes=[pltpu.SMEM((2, 2, tm), jnp.int32),
                            pltpu.VMEM((2, tm, d), F32), pltpu.VMEM((2, tm, d), F32),
                            pltpu.SemaphoreType.DMA(()), pltpu.SemaphoreType.DMA((2,)),
                            pltpu.SemaphoreType.DMA((2,))],
        ),
        out_shape=jax.ShapeDtypeStruct((2 * n + N_EXPERTS * tm, d), F32),
        compiler_params=_cparams(("arbitrary",)),
        name="moe_sparse",
    )(tile_e, n_used, g, x, idx, wg16, wu16, wd16)


def _moe_combine_kernel(x_ref, y1_ref, y2_ref, route_ref, o_ref):
    r = route_ref[...]
    o_ref[...] = x_ref[...] + r[:, 2:3] * y1_ref[...] + r[:, 3:4] * y2_ref[...]


def _moe_combine(x, y, route):
    n, d = x.shape
    tm = min(n, 512)
    nb = n // tm
    row = lambda i: (i, 0)
    return pl.pallas_call(
        _moe_combine_kernel,
        grid=(nb,),
        in_specs=[pl.BlockSpec((tm, d), row), pl.BlockSpec((tm, d), row),
                  pl.BlockSpec((tm, d), lambda i: (i + nb, 0)), pl.BlockSpec((tm, LANES), row)],
        out_specs=pl.BlockSpec((tm, d), row),
        out_shape=jax.ShapeDtypeStruct((n, d), F32),
        compiler_params=_cparams(("parallel",)),
        name="moe_combine",
    )(x, y, y, route)


PAGES_PER_STEP = 8


def _head_row_mask(width, lanes_per_row):
    lane = lax.broadcasted_iota(jnp.int32, (8, width), 1)
    row = lax.broadcasted_iota(jnp.int32, (8, width), 0)
    return jnp.logical_and(lane >= row * lanes_per_row, lane < (row + 1) * lanes_per_row)


def _sb_decode_kernel(pt_ref, q_ref, u2_ref, kt_hbm, vt_hbm, o_ref, kbuf, vbuf, sem, acc_ref, car_ref,
                      *, layer, n_pages):
    b = pl.program_id(0)
    width = q_ref.shape[2]
    q = jnp.broadcast_to(q_ref[0].astype(F32), (8, width))
    qh = jnp.where(_head_row_mask(width, SB_HEAD_DIM), q, 0.0).astype(BF16)
    acc_ref[...] = jnp.zeros_like(acc_ref)
    car_ref[...] = jnp.zeros_like(car_ref)

    def page_copies(slot):
        page = pt_ref[b * n_pages + slot]
        return (pltpu.make_async_copy(kt_hbm.at[layer, page], kbuf, sem.at[0]),
                pltpu.make_async_copy(vt_hbm.at[layer, page], vbuf, sem.at[1]))

    def cond(st):
        slot, mx = st
        return jnp.logical_and(slot >= 0, mx > EXP_ZERO_ARG)

    def body(st):
        slot, _ = st
        copies = page_copies(slot)
        for c in copies:
            c.start()
        for c in copies:
            c.wait()
        kt = kbuf[...].reshape(width, LANES).astype(BF16)
        vt = vbuf[...].reshape(width, LANES).astype(BF16)
        z = jnp.dot(qh, kt, preferred_element_type=F32)
        sp = _softplus(z)
        hi, lo = _split_hi_lo(-sp)
        r = jnp.dot(jnp.concatenate([hi, lo], axis=1), u2_ref[...], preferred_element_type=F32)
        car = car_ref[...]
        w = jnp.exp((z - sp) + r[:, :LANES] + car)
        acc_ref[...] += lax.dot_general(w.astype(BF16), vt, (((1,), (1,)), ((), ())),
                                        preferred_element_type=F32)
        car = car + r[:, LANES:]
        car_ref[...] = car
        return slot - 1, jnp.max(car)

    lax.while_loop(cond, body, (jnp.int32(n_pages - 1), jnp.float32(0.0)))
    keep = _head_row_mask(width, SB_HEAD_DIM)
    o_ref[0] = jnp.sum(jnp.where(keep, acc_ref[...], 0.0), axis=0, keepdims=True)


def _page_specs(layer, n_pages, page_shape):
    specs = []
    for i in range(PAGES_PER_STEP):
        def imap(b, j, pt, i=i):
            return (layer, pt[b * n_pages + j * PAGES_PER_STEP + i]) + (0,) * len(page_shape)
        specs.append(pl.BlockSpec((1, 1) + tuple(page_shape), imap))
    return specs


def _sb_decode(q, cache_kt, cache_vt, pt_flat, u2, layer, n_pages):
    b, _, width = q.shape
    page_shape = cache_kt.shape[2:]
    assert page_shape[2] == LANES and page_shape[0] * page_shape[1] == width
    qspec = pl.BlockSpec((1, 1, width), lambda i, pt: (i, 0, 0))
    hbm = pl.BlockSpec(memory_space=pl.ANY)
    return pl.pallas_call(
        functools.partial(_sb_decode_kernel, layer=layer, n_pages=n_pages),
        grid_spec=pltpu.PrefetchScalarGridSpec(
            num_scalar_prefetch=1,
            grid=(b,),
            in_specs=[qspec, pl.BlockSpec(u2.shape, lambda i, pt: (0, 0)), hbm, hbm],
            out_specs=qspec,
            scratch_shapes=[pltpu.VMEM(page_shape, F32), pltpu.VMEM(page_shape, F32),
                            pltpu.SemaphoreType.DMA((2,)),
                            pltpu.VMEM((8, width), F32), pltpu.VMEM((8, LANES), F32)],
        ),
        out_shape=jax.ShapeDtypeStruct((b, 1, width), F32),
        compiler_params=_cparams(("arbitrary",)),
        name="sb_decode",
    )(pt_flat, q, u2, cache_kt, cache_vt)


def _da_rows(vec):
    w = vec.shape[1]
    lane = lax.broadcasted_iota(jnp.int32, (8, w), 1)
    row = lax.broadcasted_iota(jnp.int32, (8, w), 0)
    start = (row % 4) * LANES + (row // 4) * DA_HEAD_DIM
    keep = jnp.logical_and(lane >= start, lane < start + DA_HEAD_DIM)
    return jnp.where(keep, jnp.broadcast_to(vec, (8, w)), 0.0)


def _da_decode_kernel(pt_ref, q_ref, kn_ref, vn_ref, lq1_ref, lk1_ref, lq2_ref, lk2_ref, *refs,
                      lam_init, past_len):
    npg = PAGES_PER_STEP
    k_refs, v_refs = refs[:npg], refs[npg:2 * npg]
    o_ref, m_ref, l_ref, acc_ref = refs[2 * npg:]
    j = pl.program_id(1)
    n_heads = kn_ref.shape[1]
    qh = _da_rows(q_ref[0].astype(F32))
    row = lax.broadcasted_iota(jnp.int32, (8, LANES), 0)
    slope = jnp.exp2(-2.0 * ((row % 4) + 1).astype(F32))
    q8f = sum(qh[:, hd * LANES:(hd + 1) * LANES] for hd in range(n_heads))

    @pl.when(j == 0)
    def _():
        kn = jnp.concatenate([kn_ref[0], kn_ref[0]], axis=0)
        s_self = jnp.sum(q8f * kn, axis=1, keepdims=True)
        m_ref[...] = jnp.broadcast_to(s_self, m_ref.shape)
        l_ref[...] = jnp.ones_like(l_ref)
        acc_ref[...] = jnp.concatenate([vn_ref[0], vn_ref[0]], axis=0)

    q8 = q8f.astype(BF16)
    nt_dims = (((1,), (1,)), ((), ()))
    s = jnp.concatenate(
        [lax.dot_general(q8, k_refs[i][0, 0].astype(BF16), nt_dims, preferred_element_type=F32)
         for i in range(npg)], axis=1)
    col = lax.broadcasted_iota(jnp.int32, s.shape, 1)
    row_s = lax.broadcasted_iota(jnp.int32, s.shape, 0)
    k_pos = j * (npg * LANES) + col // n_heads
    s = s - slope[:, :1] * (past_len - k_pos).astype(F32)
    s = jnp.where(col % n_heads == row_s % n_heads, s, NEG_INF)
    m_prev = m_ref[:, :1]
    m_new = jnp.maximum(m_prev, jnp.max(s, axis=1, keepdims=True))
    p = jnp.exp(s - m_new)
    alpha = jnp.exp(m_prev - m_new)
    l_ref[...] = alpha * l_ref[...] + jnp.sum(p, axis=1, keepdims=True)
    m_ref[...] = jnp.broadcast_to(m_new, m_ref.shape)
    p16 = p.astype(BF16)
    rows_pg = k_refs[0].shape[2]
    pv = None
    for i in range(npg):
        d = jnp.dot(p16[:, i * rows_pg:(i + 1) * rows_pg], v_refs[i][0, 0].astype(BF16),
                    preferred_element_type=F32)
        pv = d if pv is None else pv + d
    acc_ref[...] = alpha * acc_ref[...] + pv

    @pl.when(j == pl.num_programs(1) - 1)
    def _():
        lam = _lam_value(lq1_ref[...], lk1_ref[...], lq2_ref[...], lk2_ref[...], lam_init)
        o = acc_ref[...] / l_ref[...]
        o = o[:n_heads] - lam * o[n_heads:]
        o_ref[0] = jnp.concatenate([o[hd:hd + 1] for hd in range(n_heads)], axis=1)


def _da_decode(q, k_new, v_new, cache_k, cache_v, pt_flat, lams, layer, n_pages, lam_init):
    b, _, width = q.shape
    n_heads = width // LANES
    rows = cache_k.shape[2] // n_heads
    assert cache_k.shape[3] == LANES and rows == LANES
    qspec = pl.BlockSpec((1, 1, width), lambda i, j, pt: (i, 0, 0))
    nspec = pl.BlockSpec((1, n_heads, LANES), lambda i, j, pt: (i, 0, 0))
    lspec = pl.BlockSpec((1, DA_HEAD_DIM), lambda i, j, pt: (0, 0))
    pages = _page_specs(layer, n_pages, cache_k.shape[2:])
    as_heads = lambda t: t.reshape(b, n_heads, LANES)
    return pl.pallas_call(
        functools.partial(_da_decode_kernel, lam_init=lam_init, past_len=n_pages * rows),
        grid_spec=pltpu.PrefetchScalarGridSpec(
            num_scalar_prefetch=1,
            grid=(b, n_pages // PAGES_PER_STEP),
            in_specs=[qspec, nspec, nspec, lspec, lspec, lspec, lspec] + pages + pages,
            out_specs=qspec,
            scratch_shapes=[pltpu.VMEM((8, LANES), F32), pltpu.VMEM((8, LANES), F32),
                            pltpu.VMEM((8, LANES), F32)],
        ),
        out_shape=jax.ShapeDtypeStruct((b, 1, width), F32),
        compiler_params=_cparams(("parallel", "arbitrary")),
        name="da_decode",
    )(pt_flat, q, as_heads(k_new), as_heads(v_new), *lams,
      *([cache_k] * PAGES_PER_STEP), *([cache_v] * PAGES_PER_STEP))


def _tile_gain(g, reps):
    return jnp.tile(g.astype(F32), reps)[None, :]


def kernel(x_prompt, x_sample, mem_prompt, cache_sb_k, cache_sb_v, cache_da_k, cache_da_v, cache_mem_k, cache_mem_v, page_table, g_attn, w_in, g_da_qn, g_da_kn, lam_q1, lam_k1, lam_q2, lam_k2, g_sb_out, g_da_out, w_o, g_mem_x, g_mem_src, w_mq, w_mk, w_mv, w_mo, g_mem_qn, g_mem_kn, g_ffn, w_rg, b_rg, w_re, b_re, w_gate, w_up, w_down):
    depth = w_in.shape[0]
    bp, sp, d = x_prompt.shape
    bs, ss, _ = x_sample.shape
    assert ss == 1, "the sample group decodes one new token per sequence"
    n_pages, page = page_table.shape[1], cache_sb_k.shape[2]
    n_mem = mem_prompt.shape[1]

    bd64 = _block_diag_mean(512, 64)
    bd128 = _block_diag_mean(512, 128)
    u2 = _cumsum_matrix()
    slopes = jnp.exp2(-8.0 * jnp.arange(1, 5, dtype=F32) / 4)
    pt_flat = page_table.reshape(-1).astype(jnp.int32)
    csbk = jnp.transpose(cache_sb_k, (0, 1, 3, 4, 2))
    csbv = jnp.transpose(cache_sb_v, (0, 1, 3, 4, 2))
    da_pool = lambda c: c.reshape(c.shape[0], c.shape[1], c.shape[2] * c.shape[3], c.shape[4])
    cdak, cdav = da_pool(cache_da_k), da_pool(cache_da_v)

    xp = x_prompt.reshape(bp * sp, d)
    xs = x_sample.reshape(bs, d)
    mem = mem_prompt.reshape(bp * n_mem, d)
    outs = [[] for _ in range(10)]

    for l in range(depth):
        lam_init = 0.8 - 0.6 * math.exp(-0.3 * l)
        row = lambda v: v[l].astype(F32)[None, :]
        w_in16 = w_in[l].astype(BF16)
        w_o16 = w_o[l].astype(BF16)
        w_mq16, w_mo16 = w_mq[l].astype(BF16), w_mo[l].astype(BF16)
        w_mkv16 = jnp.concatenate([w_mk[l], w_mv[l]], axis=1).astype(BF16)
        wg16, wu16, wd16 = w_gate[l].astype(BF16), w_up[l].astype(BF16), w_down[l].astype(BF16)
        w_r = jnp.zeros((d, LANES), F32).at[:, :N_GROUPS].set(w_rg[l]).at[:, N_GROUPS:N_GROUPS + N_EXPERTS].set(w_re[l])
        w_r_hi = w_r.astype(BF16)
        w_r_lo = (w_r - w_r_hi.astype(F32)).astype(BF16)
        b_r = jnp.zeros((1, LANES), F32).at[0, :N_GROUPS].set(b_rg[l]).at[0, N_GROUPS:N_GROUPS + N_EXPERTS].set(b_re[l])
        gq, gk = _tile_gain(g_da_qn[l], 8), _tile_gain(g_da_kn[l], 8)
        gsb = _tile_gain(g_sb_out[l], 8)
        gda = _tile_gain(g_da_out[l], 4) * (1.0 - lam_init)
        gmq, gmk = _tile_gain(g_mem_qn[l], 4), _tile_gain(g_mem_kn[l], 4)
        lams = (row(lam_q1), row(lam_k1), row(lam_q2), row(lam_k2))
        nwin, slopes_bound = _da_window_limits(g_da_qn[l], g_da_kn[l], slopes, sp)

        def ffn(x):
            route = _router(x, row(g_ffn), w_r_hi, w_r_lo, b_r)
            y = _moe_sparse(x, route, row(g_ffn), wg16, wu16, wd16)
            return _moe_combine(x, y, route)

        (sbk_s, sbv_s, dak_s, dav_s, sbq16, _, _, daq16, _, _) = _in_proj(xs, row(g_attn), w_in16, gq, gk, bd64)
        as1 = lambda t: t.reshape(bs, 1, 512)
        sb_o = _sb_decode(as1(sbq16), csbk, csbv, pt_flat, u2, l, n_pages)
        da_o = _da_decode(as1(daq16), as1(dak_s), as1(dav_s), cdak, cdav, pt_flat, lams, l, n_pages, lam_init)
        xs = _merge_out(sb_o.reshape(bs, 512), da_o.reshape(bs, 512), xs, gsb, gda, bd64, bd128, w_o16)
        xs = _mem_attn(xs.reshape(bs, 1, d), cache_mem_k[l].reshape(bs, n_mem, 512),
                       cache_mem_v[l].reshape(bs, n_mem, 512),
                       row(g_mem_x), w_mq16, gmq, bd128, w_mo16).reshape(bs, d)
        xs = ffn(xs)
        for dst, val in zip(outs[6:], (sbk_s, sbv_s, dak_s, dav_s)):
            dst.append(val)

        (sbk, sbv, dak, dav, sbq16, sbk16, sbv16, daq16, dak16, dav16) = _in_proj(
            xp, row(g_attn), w_in16, gq, gk, bd64)
        as3 = lambda t: t.reshape(bp, sp, 512)
        sb_o = _sb_attn(as3(sbq16), as3(sbk16), as3(sbv16), u2)
        da_o = _da_attn(as3(daq16), as3(dak16), as3(dav16), nwin, slopes_bound, lams, lam_init)
        xp = _merge_out(sb_o.reshape(-1, 512), da_o.reshape(-1, 512), xp, gsb, gda, bd64, bd128, w_o16)
        mk, mv = _mem_kv(mem, row(g_mem_src), w_mkv16, gmk, bd128)
        xp = _mem_attn(xp.reshape(bp, sp, d), mk.reshape(bp, n_mem, 512), mv.reshape(bp, n_mem, 512),
                       row(g_mem_x), w_mq16, gmq, bd128, w_mo16).reshape(-1, d)
        xp = ffn(xp)
        for dst, val in zip(outs[:6], (sbk, sbv, dak, dav, mk, mv)):
            dst.append(val)

    st =lambda lst, shape: jnp.stack(lst, axis=0).reshape(shape)
    return (xp.reshape(bp, sp, d), xs.reshape(bs, 1, d),
            st(outs[0], (depth, bp, sp, 8, 64)), st(outs[1], (depth, bp, sp, 8, 64)),
            st(outs[2], (depth, bp, sp, 4, 128)), st(outs[3], (depth, bp, sp, 4, 128)),
            st(outs[4], (depth, bp, n_mem, 4, 128)), st(outs[5], (depth, bp, n_mem, 4, 128)),
            st(outs[6], (depth, bs, 1, 8, 64)), st(outs[7], (depth, bs, 1, 8, 64)),
            st(outs[8], (depth, bs, 1, 4, 128)), st(outs[9], (depth, bs, 1, 4, 128)))
```

```python
import functools
import math

import jax
import jax.numpy as jnp
from jax import lax
from jax.experimental import pallas as pl
from jax.experimental.pallas import tpu as pltpu

F32 = jnp.float32
BF16 = jnp.bfloat16
RMS_EPS = 1e-6
NEG_INF = float("-inf")

LANES = 128
SB_HEAD_DIM = 64
DA_HEAD_DIM = 64
N_GROUPS = 4
EXPERTS_PER_GROUP = 8
N_EXPERTS = N_GROUPS * EXPERTS_PER_GROUP
EXP_ZERO_ARG = -104.0
VMEM_LIMIT = 56 * 1024 * 1024


def _cparams(sem):
    return pltpu.CompilerParams(dimension_semantics=sem, vmem_limit_bytes=VMEM_LIMIT)


def _block_diag_mean(width, chunk):
    i = jnp.arange(width) // chunk
    return jnp.where(i[:, None] == i[None, :], 1.0 / chunk, 0.0).astype(BF16)


def _cumsum_matrix():
    j = jnp.arange(2 * LANES) % LANES
    s = jnp.arange(2 * LANES)
    return jnp.where((s[None, :] >= LANES) | (j[:, None] > s[None, :]), 1.0, 0.0).astype(BF16)


def _rms(x, g):
    return x * lax.rsqrt(jnp.mean(x * x, axis=-1, keepdims=True) + RMS_EPS) * g


def _chunk_rms(t, bd, g):
    ms = jnp.dot((t * t).astype(BF16), bd, preferred_element_type=F32)
    return t * lax.rsqrt(ms + RMS_EPS) * g


def _split_hi_lo(x):
    hi = x.astype(BF16)
    lo = (x - hi.astype(F32)).astype(BF16)
    return hi, lo


def _softplus(z):
    return jnp.maximum(z, 0.0) + jnp.log(1.0 + jnp.exp(-jnp.abs(z)))


def _in_proj_kernel(x_ref, g_ref, w_ref, gq_ref, gk_ref, bd_ref,
                    sbk_ref, sbv_ref, dak_ref, dav_ref,
                    sbq16_ref, sbk16_ref, sbv16_ref, daq16_ref, dak16_ref, dav16_ref):
    h = _rms(x_ref[...], g_ref[...])
    p = jnp.dot(h.astype(BF16), w_ref[...], preferred_element_type=F32)
    sb_q, sb_k, sb_v = p[:, 0:512], p[:, 512:1024], p[:, 1024:1536]
    da_q = _chunk_rms(p[:, 1536:2048], bd_ref[...], gq_ref[...])
    da_k = _chunk_rms(p[:, 2048:2560], bd_ref[...], gk_ref[...])
    da_v = p[:, 2560:3072]
    sbk_ref[...] = sb_k
    sbv_ref[...] = sb_v
    dak_ref[...] = da_k
    dav_ref[...] = da_v
    sbq16_ref[...] = (sb_q * 0.125).astype(BF16)
    sbk16_ref[...] = sb_k.astype(BF16)
    sbv16_ref[...] = sb_v.astype(BF16)
    daq16_ref[...] = (da_q * 0.125).astype(BF16)
    dak16_ref[...] = da_k.astype(BF16)
    dav16_ref[...] = da_v.astype(BF16)


def _in_proj(x, g, w16, gq, gk, bd64):
    n, d = x.shape
    tm = min(n, 256)
    row = lambda i: (i, 0)
    fix = lambda i: (0, 0)
    o32 = jax.ShapeDtypeStruct((n, 512), F32)
    o16 = jax.ShapeDtypeStruct((n, 512), BF16)
    ospec = pl.BlockSpec((tm, 512), row)
    return pl.pallas_call(
        _in_proj_kernel,
        grid=(n // tm,),
        in_specs=[pl.BlockSpec((tm, d), row), pl.BlockSpec((1, d), fix),
                  pl.BlockSpec(w16.shape, fix), pl.BlockSpec((1, 512), fix),
                  pl.BlockSpec((1, 512), fix), pl.BlockSpec((512, 512), fix)],
        out_specs=[ospec] * 10,
        out_shape=[o32] * 4 + [o16] * 6,
        compiler_params=_cparams(("parallel",)),
        name="in_proj",
    )(x, g, w16, gq, gk, bd64)


def _pair_rows(t, in_a):
    zero = jnp.zeros_like(t)
    return jnp.concatenate([jnp.where(in_a, t, zero), jnp.where(in_a, zero, t)], axis=0)


def _sb_block(q, k, v, u2, valid2, car, in_a):
    z = lax.dot_general(q, _pair_rows(k, in_a), (((1,), (1,)), ((), ())),
                        preferred_element_type=F32)
    sp = _softplus(z)
    l1m = -sp if valid2 is None else jnp.where(valid2, -sp, 0.0)
    hi, lo = _split_hi_lo(l1m)
    r_a = jnp.dot(jnp.concatenate([hi[:, :LANES], lo[:, :LANES]], axis=1), u2,
                  preferred_element_type=F32)
    r_b = jnp.dot(jnp.concatenate([hi[:, LANES:], lo[:, LANES:]], axis=1), u2,
                  preferred_element_type=F32)
    after = jnp.concatenate([r_a[:, :LANES], r_b[:, :LANES]], axis=1)
    total = jnp.concatenate([r_a[:, LANES:], r_b[:, LANES:]], axis=1)
    w = jnp.exp((z - sp) + after + car)
    if valid2 is not None:
        w = jnp.where(valid2, w, 0.0)
    pv = jnp.dot(w.astype(BF16), _pair_rows(v, in_a), preferred_element_type=F32)
    return pv, total


SB_Q_CHUNK = 1024


def _sb_attn_kernel(q_ref, k_ref, v_ref, u2_ref, o_ref, acc_ref, car_ref):
    t = LANES
    n_pairs = q_ref.shape[2] // LANES
    nq = q_ref.shape[1] // t
    chunk = pl.program_id(1)
    lane = lax.broadcasted_iota(jnp.int32, (t, LANES), 1)
    row = lax.broadcasted_iota(jnp.int32, (t, LANES), 0)
    in_a = lane < SB_HEAD_DIM
    col_minus_row = lane - row

    def q_body(qq, carry):
        qi = chunk * nq + qq
        qs = pl.ds(pl.multiple_of(qq * t, t), t)
        acc_ref[...] = jnp.zeros_like(acc_ref)
        car_ref[...] = jnp.zeros_like(car_ref)

        def cond(st):
            kj, mx = st
            return jnp.logical_and(kj >= 0, mx > EXP_ZERO_ARG)

        def body(st):
            kj, _ = st
            ks = pl.ds(pl.multiple_of(kj * t, t), t)
            valid = col_minus_row < (qi - kj) * t
            valid2 = jnp.concatenate([valid, valid], axis=1)
            mx = None
            for p in range(n_pairs):
                ln = slice(p * LANES, (p + 1) * LANES)
                car = car_ref[p]
                pv, total = _sb_block(q_ref[0, qs, ln], k_ref[0, ks, ln], v_ref[0, ks, ln],
                                      u2_ref[...], valid2, car, in_a)
                acc_ref[p] += pv
                car = car + total
                car_ref[p] = car
                mx = jnp.max(car) if mx is None else jnp.maximum(mx, jnp.max(car))
            return kj - 1, mx

        lax.while_loop(cond, body, (qi, jnp.float32(0.0)))
        o_ref[0, qs, :] = jnp.concatenate([acc_ref[p] for p in range(n_pairs)], axis=1)
        return carry

    lax.fori_loop(0, nq, q_body, 0)


def _sb_attn(q16, k16, v16, u2):
    b, s, w = q16.shape
    qc = min(s, SB_Q_CHUNK)
    n_pairs = w // LANES
    qspec = pl.BlockSpec((1, qc, w), lambda i, c: (i, c, 0))
    kspec = pl.BlockSpec((1, s, w), lambda i, c: (i, 0, 0))
    return pl.pallas_call(
        _sb_attn_kernel,
        grid=(b, s // qc),
        in_specs=[qspec, kspec, kspec, pl.BlockSpec(u2.shape, lambda i, c: (0, 0))],
        out_specs=qspec,
        out_shape=jax.ShapeDtypeStruct((b, s, w), F32),
        scratch_shapes=[pltpu.VMEM((n_pairs, LANES, LANES), F32),
                        pltpu.VMEM((n_pairs, LANES, 2 * LANES), F32)],
        compiler_params=_cparams(("parallel", "arbitrary")),
        name="sb_attn",
    )(q16, k16, v16, u2)


def _lam_value(lq1, lk1, lq2, lk2, lam_init):
    f = lambda a, b: jnp.exp(jnp.sum(a * b, axis=-1, keepdims=True))
    return f(lq1, lk1) - f(lq2, lk2) + lam_init


FIXED_SHIFT_MAX = 60.0
DA_TQ = 256
DA_TK = 512


def _da_attn_kernel(nwin_ref, slope_ref, q_ref, k_ref, v_ref, lq1_ref, lk1_ref, lq2_ref, lk2_ref,
                    o_ref, bias_ref, m_ref, l_ref, acc_ref, *, lam_init):
    tq = min(DA_TQ, q_ref.shape[1])
    tk = min(DA_TK, q_ref.shape[1])
    nq = q_ref.shape[1] // tq
    h = pl.program_id(1)
    slope = slope_ref[h]
    n_max = nwin_ref[h]
    bound = slope_ref[slope_ref.shape[0] - 1]
    bounded = nwin_ref[nwin_ref.shape[0] - 1] == 1
    lane = lax.broadcasted_iota(jnp.int32, (tq, tk), 1)
    row = lax.broadcasted_iota(jnp.int32, (tq, tk), 0)
    col_minus_row = lane - row
    bias_ref[...] = slope * col_minus_row.astype(F32)
    in_a = lax.broadcasted_iota(jnp.int32, (tq, LANES), 1) < DA_HEAD_DIM
    lam = _lam_value(lq1_ref[...], lk1_ref[...], lq2_ref[...], lk2_ref[...], lam_init)
    nt_dims = (((1,), (1,)), ((), ()))

    def q_body(qi, carry):
        q0 = pl.multiple_of(qi * tq, tq)
        end = q0 + tq
        q = q_ref[0, pl.ds(q0, tq), :]
        zero = jnp.zeros_like(q)
        q_maps = (jnp.where(in_a, q, zero), jnp.where(in_a, zero, q))
        m_ref[...] = jnp.full_like(m_ref, NEG_INF)
        l_ref[...] = jnp.zeros_like(l_ref)
        acc_ref[...] = jnp.zeros_like(acc_ref)

        def load(t):
            hi = end - t * tk
            start = pl.multiple_of(jnp.maximum(hi - tk, 0), tq)
            shift = slope * (q0 - start).astype(F32)
            mask = lambda: jnp.logical_and(col_minus_row <= q0 - start, lane < hi - start)
            return k_ref[0, pl.ds(start, tk), :], v_ref[0, pl.ds(start, tk), :], shift, mask

        def scores(i, k, mask):
            s = lax.dot_general(q_maps[i], k, nt_dims, preferred_element_type=F32) + bias_ref[...]
            return s if mask is None else jnp.where(mask(), s, NEG_INF)

        def fixed_shift_windows(ts, masked):
            wins = [load(t) for t in ts]
            for i in range(2):
                ln = slice(i * LANES, (i + 1) * LANES)
                l_add, acc_add = None, None
                for k, v, shift, mask in wins:
                    p = jnp.exp(scores(i, k, mask if masked else None) - (bound + shift))
                    ls = jnp.sum(p, axis=1, keepdims=True)
                    pv = jnp.dot(p.astype(BF16), v, preferred_element_type=F32)
                    l_add = ls if l_add is None else l_add + ls
                    acc_add = pv if acc_add is None else acc_add + pv
                l_ref[:, ln] += l_add
                acc_ref[:, ln] += acc_add

        def running_max_window(t, c):
            k, v, shift, mask = load(t)
            for i in range(2):
                ln = slice(i * LANES, (i + 1) * LANES)
                s = scores(i, k, mask)
                m_prev = m_ref[:, ln]
                m_new = jnp.maximum(m_prev, jnp.max(s, axis=1, keepdims=True) - shift)
                p = jnp.exp(s - (m_new[:, :1] + shift))
                alpha = jnp.exp(m_prev - m_new)
                l_ref[:, ln] = alpha * l_ref[:, ln] + jnp.sum(p, axis=1, keepdims=True)
                m_ref[:, ln] = m_new
                acc_ref[:, ln] = alpha * acc_ref[:, ln] + jnp.dot(p.astype(BF16), v, preferred_element_type=F32)
            return c

        n_all = lax.div(end + (tk - 1), tk)
        n = jnp.minimum(n_all, n_max)

        def fixed_shift_tile():
            fixed_shift_windows([0], True)
            tail = jnp.logical_and(jnp.logical_and(n == n_all, n > 1), end - (n_all - 1) * tk < tk)
            n_plain = n - 1 - tail.astype(jnp.int32)

            def pair(j, c):
                fixed_shift_windows([1 + 2 * j, 2 + 2 * j], False)
                return c

            lax.fori_loop(0, lax.div(n_plain, 2), pair, 0)

            @pl.when(lax.rem(n_plain, 2) == 1)
            def _():
                fixed_shift_windows([n_plain], False)

            @pl.when(tail)
            def _():
                fixed_shift_windows([n - 1], True)

        def running_max_tile():
            lax.fori_loop(0, n, running_max_window, 0)

        lax.cond(bounded, fixed_shift_tile, running_max_tile)
        o = acc_ref[...] / l_ref[...]
        o_ref[0, pl.ds(q0, tq), :] = o[:, :LANES] - lam * o[:, LANES:]
        return carry

    lax.fori_loop(0, nq, q_body, 0)


def _da_attn(q16, k16, v16, nwin, slopes, lams, lam_init):
    b, s, w = q16.shape
    tq, tk = min(DA_TQ, s), min(DA_TK, s)
    spec = pl.BlockSpec((1, s, LANES), lambda i, p, *_: (i, 0, p))
    lspec = pl.BlockSpec((1, DA_HEAD_DIM), lambda i, p, *_: (0, 0))
    return pl.pallas_call(
        functools.partial(_da_attn_kernel, lam_init=lam_init),
        grid_spec=pltpu.PrefetchScalarGridSpec(
            num_scalar_prefetch=1,
            grid=(b, w // LANES),
            in_specs=[pl.BlockSpec(memory_space=pltpu.SMEM), spec, spec, spec,
                      lspec, lspec, lspec, lspec],
            out_specs=spec,
            scratch_shapes=[pltpu.VMEM((tq, tk), F32), pltpu.VMEM((tq, 2 * LANES), F32),
                            pltpu.VMEM((tq, 2 * LANES), F32), pltpu.VMEM((tq, 2 * LANES), F32)],
        ),
        out_shape=jax.ShapeDtypeStruct((b, s, w), F32),
        compiler_params=_cparams(("parallel", "parallel")),
        name="da_attn",
    )(nwin, slopes, q16, k16, v16, *lams)


def _da_window_limits(g_qn, g_kn, slopes, seq):
    tq, tk = min(DA_TQ, seq), min(DA_TK, seq)
    bound = 8.0 * jnp.max(jnp.abs(g_qn)) * jnp.max(jnp.abs(g_kn)) * 1.02 + 0.1
    dist = (2.0 * bound - EXP_ZERO_ARG) / slopes
    nwin = jnp.minimum(jnp.ceil((dist + tq) / tk) + 1.0, 2.0 ** 30).astype(jnp.int32)
    fixed_ok = (2.0 * bound <= FIXED_SHIFT_MAX).astype(jnp.int32)
    return (jnp.concatenate([nwin, fixed_ok[None]]),
            jnp.concatenate([slopes, bound[None].astype(F32)]))


def _merge_kernel(sb_ref, da_ref, x_ref, gsb_ref, gda_ref, bd64_ref, bd128_ref, wo_ref, o_ref):
    sb = _chunk_rms(sb_ref[...], bd64_ref[...], gsb_ref[...])
    da = _chunk_rms(da_ref[...], bd128_ref[...], gda_ref[...])
    cat = jnp.concatenate([sb, da], axis=1).astype(BF16)
    o_ref[...] = x_ref[...] + jnp.dot(cat, wo_ref[...], preferred_element_type=F32)


def _merge_out(sb_o, da_o, x, gsb, gda, bd64, bd128, wo16):
    n, d = x.shape
    tm = min(n, 512)
    row = lambda i: (i, 0)
    fix = lambda i: (0, 0)
    return pl.pallas_call(
        _merge_kernel,
        grid=(n // tm,),
        in_specs=[pl.BlockSpec((tm, 512), row), pl.BlockSpec((tm, 512), row), pl.BlockSpec((tm, d), row),
                  pl.BlockSpec((1, 512), fix), pl.BlockSpec((1, 512), fix),
                  pl.BlockSpec((512, 512), fix), pl.BlockSpec((512, 512), fix),
                  pl.BlockSpec(wo16.shape, fix)],
        out_specs=pl.BlockSpec((tm, d), row),
        out_shape=jax.ShapeDtypeStruct((n, d), F32),
        compiler_params=_cparams(("parallel",)),
        name="merge_out",
    )(sb_o, da_o, x, gsb, gda, bd64, bd128, wo16)


def _mem_kv_kernel(x_ref, g_ref, w_ref, gk_ref, bd_ref, k_ref, v_ref):
    m = _rms(x_ref[...], g_ref[...])
    p = jnp.dot(m.astype(BF16), w_ref[...], preferred_element_type=F32)
    k_ref[...] = _chunk_rms(p[:, :512], bd_ref[...], gk_ref[...])
    v_ref[...] = p[:, 512:]


def _mem_kv(mem, g, wkv16, gk, bd128):
    n, d = mem.shape
    tm = min(n, 256)
    row = lambda i: (i, 0)
    fix = lambda i: (0, 0)
    o = jax.ShapeDtypeStruct((n, 512), F32)
    return pl.pallas_call(
        _mem_kv_kernel,
        grid=(n // tm,),
        in_specs=[pl.BlockSpec((tm, d), row), pl.BlockSpec((1, d), fix), pl.BlockSpec(wkv16.shape, fix),
                  pl.BlockSpec((1, 512), fix), pl.BlockSpec((512, 512), fix)],
        out_specs=[pl.BlockSpec((tm, 512), row)] * 2,
        out_shape=[o, o],
        compiler_params=_cparams(("parallel",)),
        name="mem_kv",
    )(mem, g, wkv16, gk, bd128)


def _mem_attn_kernel(x_ref, mk_ref, mv_ref, g_ref, wq_ref, gq_ref, bd_ref, wo_ref, o_ref):
    tm = x_ref.shape[1]
    x = x_ref[0]
    if tm < 8:
        x = jnp.broadcast_to(x[:1], (8, x.shape[1]))
    h = _rms(x, g_ref[...])
    q = jnp.dot(h.astype(BF16), wq_ref[...], preferred_element_type=F32)
    q = (_chunk_rms(q, bd_ref[...], gq_ref[...]) * (LANES ** -0.5)).astype(BF16)
    mk = mk_ref[0].astype(BF16)
    mv = mv_ref[0].astype(BF16)
    outs = []
    for hd in range(4):
        sl = slice(hd * LANES, (hd + 1) * LANES)
        s = lax.dot_general(q[:, sl], mk[:, sl], (((1,), (1,)), ((), ())), preferred_element_type=F32)
        p = jnp.exp(s - jnp.max(s, axis=1, keepdims=True))
        p = p / jnp.sum(p, axis=1, keepdims=True)
        outs.append(jnp.dot(p.astype(BF16), mv[:, sl], preferred_element_type=F32))
    o = jnp.concatenate(outs, axis=1).astype(BF16)
    o_ref[0] = (x + jnp.dot(o, wo_ref[...], preferred_element_type=F32))[:tm]


def _mem_attn(x, mk, mv, g, wq16, gq, bd128, wo16):
    b, t, d = x.shape
    tm = min(t, 512)
    m = mk.shape[1]
    fix = lambda i, j: (0, 0)
    return pl.pallas_call(
        _mem_attn_kernel,
        grid=(b, t // tm),
        in_specs=[pl.BlockSpec((1, tm, d), lambda i, j: (i, j, 0)),
                  pl.BlockSpec((1, m, 512), lambda i, j: (i, 0, 0)),
                  pl.BlockSpec((1, m, 512), lambda i, j: (i, 0, 0)),
                  pl.BlockSpec((1, d), fix), pl.BlockSpec(wq16.shape, fix), pl.BlockSpec((1, 512), fix),
                  pl.BlockSpec((512, 512), fix), pl.BlockSpec(wo16.shape, fix)],
        out_specs=pl.BlockSpec((1, tm, d), lambda i, j: (i, j, 0)),
        out_shape=jax.ShapeDtypeStruct((b, t, d), F32),
        compiler_params=_cparams(("parallel", "parallel")),
        name="mem_attn",
    )(x, mk, mv, g, wq16, gq, bd128, wo16)


def _first_max(vals, lane_f):
    mx = jnp.max(vals, axis=1, keepdims=True)
    idx = jnp.min(jnp.where(vals == mx, lane_f, float(LANES)), axis=1, keepdims=True)
    return mx, idx


def _router_kernel(x_ref, g_ref, whi_ref, wlo_ref, b_ref, route_ref):
    h = _rms(x_ref[...], g_ref[...])
    hi, lo = _split_hi_lo(h)
    lg = (jnp.dot(hi, whi_ref[...], preferred_element_type=F32)
          + jnp.dot(lo, whi_ref[...], preferred_element_type=F32)
          + jnp.dot(hi, wlo_ref[...], preferred_element_type=F32)) + b_ref[...]
    lane = lax.broadcasted_iota(jnp.int32, lg.shape, 1)
    lane_f = lane.astype(F32)
    is_g = lane < N_GROUPS
    gl = jnp.where(is_g, lg, NEG_INF)
    g_max, g_sel = _first_max(gl, lane_f)
    p_g = 1.0 / jnp.sum(jnp.where(is_g, jnp.exp(gl - g_max), 0.0), axis=1, keepdims=True)
    e_lo = N_GROUPS + g_sel * EXPERTS_PER_GROUP
    is_e = jnp.logical_and(lane_f >= e_lo, lane_f < e_lo + EXPERTS_PER_GROUP)
    el = jnp.where(is_e, lg, NEG_INF)
    t1, i1 = _first_max(el, lane_f)
    t2, i2 = _first_max(jnp.where(lane_f == i1, NEG_INF, el), lane_f)
    e2 = jnp.exp(t2 - t1)
    w1 = p_g / (1.0 + e2)
    route_ref[...] = jnp.where(lane == 0, i1 - N_GROUPS,
                               jnp.where(lane == 1, i2 - N_GROUPS,
                                         jnp.where(lane == 2, w1, jnp.where(lane == 3, w1 * e2, 0.0))))


def _router(x, g, whi, wlo, bias):
    n, d = x.shape
    tm = min(n, 512)
    row = lambda i: (i, 0)
    fix = lambda i: (0, 0)
    return pl.pallas_call(
        _router_kernel,
        grid=(n // tm,),
        in_specs=[pl.BlockSpec((tm, d), row), pl.BlockSpec((1, d), fix),
                  pl.BlockSpec((d, LANES), fix), pl.BlockSpec((d, LANES), fix), pl.BlockSpec((1, LANES), fix)],
        out_specs=pl.BlockSpec((tm, LANES), row),
        out_shape=jax.ShapeDtypeStruct((n, LANES), F32),
        compiler_params=_cparams(("parallel",)),
        name="router",
    )(x, g, whi, wlo, bias)


MOE_TM = 256


def _route_plan(route, tm):
    n = route.shape[0]
    n_tiles = (2 * n) // tm + N_EXPERTS
    e_flat = route[:, :2].astype(jnp.int32).reshape(-1)
    order = jnp.argsort(e_flat, stable=True).astype(jnp.int32)
    counts = jnp.sum(e_flat[:, None] == jnp.arange(N_EXPERTS, dtype=jnp.int32)[None, :], axis=0,
                     dtype=jnp.int32)
    tiles_e = (counts + tm - 1) // tm
    tile_end = jnp.cumsum(tiles_e)
    tile_beg = tile_end - tiles_e
    pair_beg = jnp.cumsum(counts) - counts
    j = jnp.arange(n_tiles, dtype=jnp.int32)
    tile_e = jnp.sum(j[:, None] >= tile_end[None, :], axis=1, dtype=jnp.int32)
    te = jnp.minimum(tile_e, N_EXPERTS - 1)
    off = (j - tile_beg[te])[:, None] * tm + jnp.arange(tm, dtype=jnp.int32)[None, :]
    used = (tile_e < N_EXPERTS)[:, None]
    ok = jnp.logical_and(off < counts[te][:, None], used)
    pair = order[jnp.clip(pair_beg[te][:, None] + off, 0, 2 * n - 1)]
    pad = jnp.logical_and(jnp.logical_not(ok), used)
    spare = 2 * n + jnp.cumsum(pad.reshape(-1).astype(jnp.int32)).reshape(pad.shape) - 1
    src = jnp.where(ok, pair // 2, 0)
    dst = jnp.where(ok, (pair % 2) * n + pair // 2, jnp.where(pad, spare, 0))
    idx = jnp.stack([src, dst], axis=1).astype(jnp.int32)
    return tile_e, tile_end[-1:].astype(jnp.int32), idx


def _moe_sparse_kernel(te_ref, nv_ref, g_ref, x_hbm, idx_hbm, wg_ref, wu_ref, wd_ref, y_hbm,
                       idx_smem, xbuf, ybuf, isem, gsem, ssem):
    i = pl.program_id(0)
    n_used = nv_ref[0]
    slot = i % 2
    tm = xbuf.shape[1]

    def idx_copy(tile, s):
        return pltpu.make_async_copy(idx_hbm.at[tile], idx_smem.at[s], isem)

    def gather_row(s, r):
        tok = idx_smem[s, 0, r]
        return pltpu.make_async_copy(x_hbm.at[pl.ds(tok, 1), :], xbuf.at[s, pl.ds(r, 1), :], gsem.at[s])

    def scatter_row(s, r):
        dst = idx_smem[s, 1, r]
        return pltpu.make_async_copy(ybuf.at[s, pl.ds(r, 1), :], y_hbm.at[pl.ds(dst, 1), :], ssem.at[s])

    def gather_all(s):
        return pltpu.make_async_copy(x_hbm.at[pl.ds(0, tm), :], xbuf.at[s], gsem.at[s])

    def scatter_all(s):
        return pltpu.make_async_copy(ybuf.at[s], y_hbm.at[pl.ds(0, tm), :], ssem.at[s])

    def issue(make_row, s):
        def body(r, c):
            make_row(s, r).start()
            return c
        lax.fori_loop(0, tm, body, 0, unroll=8)

    @pl.when(i == 0)
    def _():
        xbuf[1] = jnp.zeros(xbuf.shape[1:], F32)
        spare0 = y_hbm.shape[0] - N_EXPERTS * tm
        fills = [pltpu.make_async_copy(xbuf.at[1], y_hbm.at[pl.ds(spare0 + e * tm, tm), :], ssem.at[1])
                 for e in range(N_EXPERTS)]
        for c in fills:
            c.start()
        for c in fills:
            c.wait()
        idx_copy(0, 0).start()
        idx_copy(0, 0).wait()
        issue(gather_row, 0)

    @pl.when(i < n_used)
    def _():
        more = i + 1 < n_used

        @pl.when(more)
        def _():
            idx_copy(i + 1, 1 - slot).start()

        gather_all(slot).wait()

        @pl.when(more)
        def _():
            idx_copy(i + 1, 1 - slot).wait()
            issue(gather_row, 1 - slot)

        h = _rms(xbuf[slot], g_ref[...]).astype(BF16)
        g = jnp.dot(h, wg_ref[0], preferred_element_type=F32)
        u = jnp.dot(h, wu_ref[0], preferred_element_type=F32)
        a = (g * jax.nn.sigmoid(g)) * u
        y = jnp.dot(a.astype(BF16), wd_ref[0], preferred_element_type=F32)

        @pl.when(i >= 2)
        def _():
            scatter_all(slot).wait()

        ybuf[slot] = y
        issue(scatter_row, slot)

        @pl.when(i == n_used - 1)
        def _():
            scatter_all(slot).wait()

            @pl.when(i >= 1)
            def _():
                scatter_all(1 - slot).wait()


def _moe_sparse(x, route, g, wg16, wu16, wd16):
    n, d = x.shape
    f = wg16.shape[2]
    tm = min(MOE_TM, n)
    tile_e, n_used, idx = _route_plan(route, tm)
    n_tiles = idx.shape[0]
    wmap = lambda i, te, nv: (jnp.minimum(te[i], N_EXPERTS - 1), 0, 0)
    hbm = pl.BlockSpec(memory_space=pl.ANY)
    return pl.pallas_call(
        _moe_sparse_kernel,
        grid_spec=pltpu.PrefetchScalarGridSpec(
            num_scalar_prefetch=2,
            grid=(n_tiles,),
            in_specs=[pl.BlockSpec((1, d), lambda i, te, nv: (0, 0)), hbm, hbm,
                      pl.BlockSpec((1, d, f), wmap), pl.BlockSpec((1, d, f), wmap),
                      pl.BlockSpec((1, f, d), wmap)],
            out_specs=hbm,
            scratch_shapes=[pltpu.SMEM((2, 2, tm), jnp.int32),
                            pltpu.VMEM((2, tm, d), F32), pltpu.VMEM((2, tm, d), F32),
                            pltpu.SemaphoreType.DMA(()), pltpu.SemaphoreType.DMA((2,)),
                            pltpu.SemaphoreType.DMA((2,))],
        ),
        out_shape=jax.ShapeDtypeStruct((2 * n + N_EXPERTS * tm, d), F32),
        compiler_params=_cparams(("arbitrary",)),
        name="moe_sparse",
    )(tile_e, n_used, g, x, idx, wg16, wu16, wd16)


def _moe_combine_kernel(x_ref, y1_ref, y2_ref, route_ref, o_ref):
    r = route_ref[...]
    o_ref[...] = x_ref[...] + r[:, 2:3] * y1_ref[...] + r[:, 3:4] * y2_ref[...]


def _moe_combine(x, y, route):
    n, d = x.shape
    tm = min(n, 512)
    nb = n // tm
    row = lambda i: (i, 0)
    return pl.pallas_call(
        _moe_combine_kernel,
        grid=(nb,),
        in_specs=[pl.BlockSpec((tm, d), row), pl.BlockSpec((tm, d), row),
                  pl.BlockSpec((tm, d), lambda i: (i + nb, 0)), pl.BlockSpec((tm, LANES), row)],
        out_specs=pl.BlockSpec((tm, d), row),
        out_shape=jax.ShapeDtypeStruct((n, d), F32),
        compiler_params=_cparams(("parallel",)),
        name="moe_combine",
    )(x, y, y, route)


PAGES_PER_STEP = 16


def _head_row_mask(width, lanes_per_row):
    lane = lax.broadcasted_iota(jnp.int32, (8, width), 1)
    row = lax.broadcasted_iota(jnp.int32, (8, width), 0)
    return jnp.logical_and(lane >= row * lanes_per_row, lane < (row + 1) * lanes_per_row)


def _sb_decode_kernel(pt_ref, q_ref, u2_ref, kt_hbm, vt_hbm, o_ref, kbuf, vbuf, sem, acc_ref, car_ref,
                      *, layer, n_pages):
    b = pl.program_id(0)
    width = q_ref.shape[2]
    q = jnp.broadcast_to(q_ref[0].astype(F32), (8, width))
    qh = jnp.where(_head_row_mask(width, SB_HEAD_DIM), q, 0.0).astype(BF16)
    acc_ref[...] = jnp.zeros_like(acc_ref)
    car_ref[...] = jnp.zeros_like(car_ref)

    def page_copies(slot):
        page = pt_ref[b * n_pages + slot]
        return (pltpu.make_async_copy(kt_hbm.at[layer, page], kbuf, sem.at[0]),
                pltpu.make_async_copy(vt_hbm.at[layer, page], vbuf, sem.at[1]))

    def cond(st):
        slot, mx = st
        return jnp.logical_and(slot >= 0, mx > EXP_ZERO_ARG)

    def body(st):
        slot, _ = st
        copies = page_copies(slot)
        for c in copies:
            c.start()
        for c in copies:
            c.wait()
        kt = kbuf[...].reshape(width, LANES).astype(BF16)
        vt = vbuf[...].reshape(width, LANES).astype(BF16)
        z = jnp.dot(qh, kt, preferred_element_type=F32)
        sp = _softplus(z)
        hi, lo = _split_hi_lo(-sp)
        r = jnp.dot(jnp.concatenate([hi, lo], axis=1), u2_ref[...], preferred_element_type=F32)
        car = car_ref[...]
        w = jnp.exp((z - sp) + r[:, :LANES] + car)
        acc_ref[...] += lax.dot_general(w.astype(BF16), vt, (((1,), (1,)), ((), ())),
                                        preferred_element_type=F32)
        car = car + r[:, LANES:]
        car_ref[...] = car
        return slot - 1, jnp.max(car)

    lax.while_loop(cond, body, (jnp.int32(n_pages - 1), jnp.float32(0.0)))
    keep = _head_row_mask(width, SB_HEAD_DIM)
    o_ref[0] = jnp.sum(jnp.where(keep, acc_ref[...], 0.0), axis=0, keepdims=True)


def _page_specs(layer, n_pages, page_shape):
    specs = []
    for i in range(PAGES_PER_STEP):
        def imap(b, j, pt, i=i):
            return (layer, pt[b * n_pages + j * PAGES_PER_STEP + i]) + (0,) * len(page_shape)
        specs.append(pl.BlockSpec((1, 1) + tuple(page_shape), imap))
    return specs


def _sb_decode(q, cache_kt, cache_vt, pt_flat, u2, layer, n_pages):
    b, _, width = q.shape
    page_shape = cache_kt.shape[2:]
    assert page_shape[2] == LANES and page_shape[0] * page_shape[1] == width
    qspec = pl.BlockSpec((1, 1, width), lambda i, pt: (i, 0, 0))
    hbm = pl.BlockSpec(memory_space=pl.ANY)
    return pl.pallas_call(
        functools.partial(_sb_decode_kernel, layer=layer, n_pages=n_pages),
        grid_spec=pltpu.PrefetchScalarGridSpec(
            num_scalar_prefetch=1,
            grid=(b,),
            in_specs=[qspec, pl.BlockSpec(u2.shape, lambda i, pt: (0, 0)), hbm, hbm],
            out_specs=qspec,
            scratch_shapes=[pltpu.VMEM(page_shape, F32), pltpu.VMEM(page_shape, F32),
                            pltpu.SemaphoreType.DMA((2,)),
                            pltpu.VMEM((8, width), F32), pltpu.VMEM((8, LANES), F32)],
        ),
        out_shape=jax.ShapeDtypeStruct((b, 1, width), F32),
        compiler_params=_cparams(("arbitrary",)),
        name="sb_decode",
    )(pt_flat, q, u2, cache_kt, cache_vt)


def _da_rows(vec):
    w = vec.shape[1]
    lane = lax.broadcasted_iota(jnp.int32, (8, w), 1)
    row = lax.broadcasted_iota(jnp.int32, (8, w), 0)
    start = (row % 4) * LANES + (row // 4) * DA_HEAD_DIM
    keep = jnp.logical_and(lane >= start, lane < start + DA_HEAD_DIM)
    return jnp.where(keep, jnp.broadcast_to(vec, (8, w)), 0.0)


def _da_decode_kernel(pt_ref, q_ref, kn_ref, vn_ref, lq1_ref, lk1_ref, lq2_ref, lk2_ref, *refs,
                      lam_init, past_len):
    npg = PAGES_PER_STEP
    k_refs, v_refs = refs[:npg], refs[npg:2 * npg]
    o_ref, m_ref, l_ref, acc_ref = refs[2 * npg:]
    j = pl.program_id(1)
    n_heads = kn_ref.shape[1]
    qh = _da_rows(q_ref[0].astype(F32))
    row = lax.broadcasted_iota(jnp.int32, (8, LANES), 0)
    slope = jnp.exp2(-2.0 * ((row % 4) + 1).astype(F32))
    q8f = sum(qh[:, hd * LANES:(hd + 1) * LANES] for hd in range(n_heads))

    @pl.when(j == 0)
    def _():
        kn = jnp.concatenate([kn_ref[0], kn_ref[0]], axis=0)
        s_self = jnp.sum(q8f * kn, axis=1, keepdims=True)
        m_ref[...] = jnp.broadcast_to(s_self, m_ref.shape)
        l_ref[...] = jnp.ones_like(l_ref)
        acc_ref[...] = jnp.concatenate([vn_ref[0], vn_ref[0]], axis=0)

    q8 = q8f.astype(BF16)
    nt_dims = (((1,), (1,)), ((), ()))
    s = jnp.concatenate(
        [lax.dot_general(q8, k_refs[i][0, 0].astype(BF16), nt_dims, preferred_element_type=F32)
         for i in range(npg)], axis=1)
    col = lax.broadcasted_iota(jnp.int32, s.shape, 1)
    row_s = lax.broadcasted_iota(jnp.int32, s.shape, 0)
    k_pos = j * (npg * LANES) + col // n_heads
    s = s - slope[:, :1] * (past_len - k_pos).astype(F32)
    s = jnp.where(col % n_heads == row_s % n_heads, s, NEG_INF)
    m_prev = m_ref[:, :1]
    m_new = jnp.maximum(m_prev, jnp.max(s, axis=1, keepdims=True))
    p = jnp.exp(s - m_new)
    alpha = jnp.exp(m_prev - m_new)
    l_ref[...] = alpha * l_ref[...] + jnp.sum(p, axis=1, keepdims=True)
    m_ref[...] = jnp.broadcast_to(m_new, m_ref.shape)
    p16 = p.astype(BF16)
    rows_pg = k_refs[0].shape[2]
    pv = None
    for i in range(npg):
        d = jnp.dot(p16[:, i * rows_pg:(i + 1) * rows_pg], v_refs[i][0, 0].astype(BF16),
                    preferred_element_type=F32)
        pv = d if pv is None else pv + d
    acc_ref[...] = alpha * acc_ref[...] + pv

    @pl.when(j == pl.num_programs(1) - 1)
    def _():
        lam = _lam_value(lq1_ref[...], lk1_ref[...], lq2_ref[...], lk2_ref[...], lam_init)
        o = acc_ref[...] / l_ref[...]
        o = o[:n_heads] - lam * o[n_heads:]
        o_ref[0] = jnp.concatenate([o[hd:hd + 1] for hd in range(n_heads)], axis=1)


def _da_decode(q, k_new, v_new, cache_k, cache_v, pt_flat, lams, layer, n_pages, lam_init):
    b, _, width = q.shape
    n_heads = width // LANES
    rows = cache_k.shape[2] // n_heads
    assert cache_k.shape[3] == LANES and rows == LANES
    qspec = pl.BlockSpec((1, 1, width), lambda i, j, pt: (i, 0, 0))
    nspec = pl.BlockSpec((1, n_heads, LANES), lambda i, j, pt: (i, 0, 0))
    lspec = pl.BlockSpec((1, DA_HEAD_DIM), lambda i, j, pt: (0, 0))
    pages = _page_specs(layer, n_pages, cache_k.shape[2:])
    as_heads = lambda t: t.reshape(b, n_heads, LANES)
    return pl.pallas_call(
        functools.partial(_da_decode_kernel, lam_init=lam_init, past_len=n_pages * rows),
        grid_spec=pltpu.PrefetchScalarGridSpec(
            num_scalar_prefetch=1,
            grid=(b, n_pages // PAGES_PER_STEP),
            in_specs=[qspec, nspec, nspec, lspec, lspec, lspec, lspec] + pages + pages,
            out_specs=qspec,
            scratch_shapes=[pltpu.VMEM((8, LANES), F32), pltpu.VMEM((8, LANES), F32),
                            pltpu.VMEM((8, LANES), F32)],
        ),
        out_shape=jax.ShapeDtypeStruct((b, 1, width), F32),
        compiler_params=_cparams(("parallel", "arbitrary")),
        name="da_decode",
    )(pt_flat, q, as_heads(k_new), as_heads(v_new), *lams,
      *([cache_k] * PAGES_PER_STEP), *([cache_v] * PAGES_PER_STEP))


def _tile_gain(g, reps):
    return jnp.tile(g.astype(F32), reps)[None, :]


def kernel(x_prompt, x_sample, mem_prompt, cache_sb_k, cache_sb_v, cache_da_k, cache_da_v, cache_mem_k, cache_mem_v, page_table, g_attn, w_in, g_da_qn, g_da_kn, lam_q1, lam_k1, lam_q2, lam_k2, g_sb_out, g_da_out, w_o, g_mem_x, g_mem_src, w_mq, w_mk, w_mv, w_mo, g_mem_qn, g_mem_kn, g_ffn, w_rg, b_rg, w_re, b_re, w_gate, w_up, w_down):
    depth = w_in.shape[0]
    bp, sp, d = x_prompt.shape
    bs, ss, _ = x_sample.shape
    assert ss == 1, "the sample group decodes one new token per sequence"
    n_pages, page = page_table.shape[1], cache_sb_k.shape[2]
    n_mem = mem_prompt.shape[1]

    bd64 = _block_diag_mean(512, 64)
    bd128 = _block_diag_mean(512, 128)
    u2 = _cumsum_matrix()
    slopes = jnp.exp2(-8.0 * jnp.arange(1, 5, dtype=F32) / 4)
    pt_flat = page_table.reshape(-1).astype(jnp.int32)
    csbk = jnp.transpose(cache_sb_k, (0, 1, 3, 4, 2))
    csbv = jnp.transpose(cache_sb_v, (0, 1, 3, 4, 2))
    da_pool = lambda c: c.reshape(c.shape[0], c.shape[1], c.shape[2] * c.shape[3], c.shape[4])
    cdak, cdav = da_pool(cache_da_k), da_pool(cache_da_v)

    xp = x_prompt.reshape(bp * sp, d)
    xs = x_sample.reshape(bs, d)
    mem = mem_prompt.reshape(bp * n_mem, d)
    outs = [[] for _ in range(10)]

    for l in range(depth):
        lam_init = 0.8 - 0.6 * math.exp(-0.3 * l)
        row = lambda v: v[l].astype(F32)[None, :]
        w_in16 = w_in[l].astype(BF16)
        w_o16 = w_o[l].astype(BF16)
        w_mq16, w_mo16 = w_mq[l].astype(BF16), w_mo[l].astype(BF16)
        w_mkv16 = jnp.concatenate([w_mk[l], w_mv[l]], axis=1).astype(BF16)
        wg16, wu16, wd16 = w_gate[l].astype(BF16), w_up[l].astype(BF16), w_down[l].astype(BF16)
        w_r = jnp.zeros((d, LANES), F32).at[:, :N_GROUPS].set(w_rg[l]).at[:, N_GROUPS:N_GROUPS + N_EXPERTS].set(w_re[l])
        w_r_hi = w_r.astype(BF16)
        w_r_lo = (w_r - w_r_hi.astype(F32)).astype(BF16)
        b_r = jnp.zeros((1, LANES), F32).at[0, :N_GROUPS].set(b_rg[l]).at[0, N_GROUPS:N_GROUPS + N_EXPERTS].set(b_re[l])
        gq, gk = _tile_gain(g_da_qn[l], 8), _tile_gain(g_da_kn[l], 8)
        gsb = _tile_gain(g_sb_out[l], 8)
        gda = _tile_gain(g_da_out[l], 4) * (1.0 - lam_init)
        gmq, gmk = _tile_gain(g_mem_qn[l], 4), _tile_gain(g_mem_kn[l], 4)
        lams = (row(lam_q1), row(lam_k1), row(lam_q2), row(lam_k2))
        nwin, slopes_bound = _da_window_limits(g_da_qn[l], g_da_kn[l], slopes, sp)

        def ffn(x):
            route = _router(x, row(g_ffn), w_r_hi, w_r_lo, b_r)
            y = _moe_sparse(x, route, row(g_ffn), wg16, wu16, wd16)
            return _moe_combine(x, y, route)

        (sbk_s, sbv_s, dak_s, dav_s, sbq16, _, _, daq16, _, _) = _in_proj(xs, row(g_attn), w_in16, gq, gk, bd64)
        as1 = lambda t: t.reshape(bs, 1, 512)
        sb_o = _sb_decode(as1(sbq16), csbk, csbv, pt_flat, u2, l, n_pages)
        da_o = _da_decode(as1(daq16), as1(dak_s), as1(dav_s), cdak, cdav, pt_flat, lams, l, n_pages, lam_init)
        xs = _merge_out(sb_o.reshape(bs, 512), da_o.reshape(bs, 512), xs, gsb, gda, bd64, bd128, w_o16)
        xs = _mem_attn(xs.reshape(bs, 1, d), cache_mem_k[l].reshape(bs, n_mem, 512),
                       cache_mem_v[l].reshape(bs, n_mem, 512),
                       row(g_mem_x), w_mq16, gmq, bd128, w_mo16).reshape(bs, d)
        xs = ffn(xs)
        for dst, val in zip(outs[6:], (sbk_s, sbv_s, dak_s, dav_s)):
            dst.append(val)

        (sbk, sbv, dak, dav, sbq16, sbk16, sbv16, daq16, dak16, dav16) = _in_proj(
            xp, row(g_attn), w_in16, gq, gk, bd64)
        as3 = lambda t: t.reshape(bp, sp, 512)
        sb_o = _sb_attn(as3(sbq16), as3(sbk16), as3(sbv16), u2)
        da_o = _da_attn(as3(daq16), as3(dak16), as3(dav16), nwin, slopes_bound, lams, lam_init)
        xp = _merge_out(sb_o.reshape(-1, 512), da_o.reshape(-1, 512), xp, gsb, gda, bd64, bd128, w_o16)
        mk, mv = _mem_kv(mem, row(g_mem_src), w_mkv16, gmk, bd128)
        xp = _mem_attn(xp.reshape(bp, sp, d), mk.reshape(bp, n_mem, 512), mv.reshape(bp, n_mem, 512),
                       row(g_mem_x), w_mq16, gmq, bd128, w_mo16).reshape(-1, d)
        xp = ffn(xp)
        for dst, val in zip(outs[:6], (sbk, sbv, dak, dav, mk, mv)):
            dst.append(val)

    st =lambda lst, shape: jnp.stack(lst, axis=0).reshape(shape)
    return (xp.reshape(bp, sp, d), xs.reshape(bs, 1, d),
            st(outs[0], (depth, bp, sp, 8, 64)), st(outs[1], (depth, bp, sp, 8, 64)),
            st(outs[2], (depth, bp, sp, 4, 128)), st(outs[3], (depth, bp, sp, 4, 128)),
            st(outs[4], (depth, bp, n_mem, 4, 128)), st(outs[5], (depth, bp, n_mem, 4, 128)),
            st(outs[6], (depth, bs, 1, 8, 64)), st(outs[7], (depth, bs, 1, 8, 64)),
            st(outs[8], (depth, bs, 1, 4, 128)), st(outs[9], (depth, bs, 1, 4, 128)))
```

```python
import functools
import math

import jax
import jax.numpy as jnp
from jax import lax
from jax.experimental import pallas as pl
from jax.experimental.pallas import tpu as pltpu

F32 = jnp.float32
BF16 = jnp.bfloat16
RMS_EPS = 1e-6
NEG_INF = float("-inf")

LANES = 128
SB_HEAD_DIM = 64
DA_HEAD_DIM = 64
N_GROUPS = 4
EXPERTS_PER_GROUP = 8
N_EXPERTS = N_GROUPS * EXPERTS_PER_GROUP
EXP_ZERO_ARG = -104.0
VMEM_LIMIT = 56 * 1024 * 1024


def _cparams(sem):
    return pltpu.CompilerParams(dimension_semantics=sem, vmem_limit_bytes=VMEM_LIMIT)


def _block_diag_mean(width, chunk):
    i = jnp.arange(width) // chunk
    return jnp.where(i[:, None] == i[None, :], 1.0 / chunk, 0.0).astype(BF16)


def _cumsum_matrix():
    j = jnp.arange(2 * LANES) % LANES
    s = jnp.arange(2 * LANES)
    return jnp.where((s[None, :] >= LANES) | (j[:, None] > s[None, :]), 1.0, 0.0).astype(BF16)


def _rms(x, g):
    return x * lax.rsqrt(jnp.mean(x * x, axis=-1, keepdims=True) + RMS_EPS) * g


def _chunk_rms(t, bd, g):
    ms = jnp.dot((t * t).astype(BF16), bd, preferred_element_type=F32)
    return t * lax.rsqrt(ms + RMS_EPS) * g


def _split_hi_lo(x):
    hi = x.astype(BF16)
    lo = (x - hi.astype(F32)).astype(BF16)
    return hi, lo


def _softplus(z):
    return jnp.maximum(z, 0.0) + jnp.log(1.0 + jnp.exp(-jnp.abs(z)))


def _in_proj_kernel(x_ref, g_ref, w_ref, gq_ref, gk_ref, bd_ref,
                    sbk_ref, sbv_ref, dak_ref, dav_ref,
                    sbq16_ref, sbk16_ref, sbv16_ref, daq16_ref, dak16_ref, dav16_ref):
    h = _rms(x_ref[...], g_ref[...])
    p = jnp.dot(h.astype(BF16), w_ref[...], preferred_element_type=F32)
    sb_q, sb_k, sb_v = p[:, 0:512], p[:, 512:1024], p[:, 1024:1536]
    da_q = _chunk_rms(p[:, 1536:2048], bd_ref[...], gq_ref[...])
    da_k = _chunk_rms(p[:, 2048:2560], bd_ref[...], gk_ref[...])
    da_v = p[:, 2560:3072]
    sbk_ref[...] = sb_k
    sbv_ref[...] = sb_v
    dak_ref[...] = da_k
    dav_ref[...] = da_v
    sbq16_ref[...] = (sb_q * 0.125).astype(BF16)
    sbk16_ref[...] = sb_k.astype(BF16)
    sbv16_ref[...] = sb_v.astype(BF16)
    daq16_ref[...] = (da_q * 0.125).astype(BF16)
    dak16_ref[...] = da_k.astype(BF16)
    dav16_ref[...] = da_v.astype(BF16)


def _in_proj(x, g, w16, gq, gk, bd64):
    n, d = x.shape
    tm = min(n, 256)
    row = lambda i: (i, 0)
    fix = lambda i: (0, 0)
    o32 = jax.ShapeDtypeStruct((n, 512), F32)
    o16 = jax.ShapeDtypeStruct((n, 512), BF16)
    ospec = pl.BlockSpec((tm, 512), row)
    return pl.pallas_call(
        _in_proj_kernel,
        grid=(n // tm,),
        in_specs=[pl.BlockSpec((tm, d), row), pl.BlockSpec((1, d), fix),
                  pl.BlockSpec(w16.shape, fix), pl.BlockSpec((1, 512), fix),
                  pl.BlockSpec((1, 512), fix), pl.BlockSpec((512, 512), fix)],
        out_specs=[ospec] * 10,
        out_shape=[o32] * 4 + [o16] * 6,
        compiler_params=_cparams(("parallel",)),
        name="in_proj",
    )(x, g, w16, gq, gk, bd64)


def _pair_rows(t, in_a):
    zero = jnp.zeros_like(t)
    return jnp.concatenate([jnp.where(in_a, t, zero), jnp.where(in_a, zero, t)], axis=0)


def _sb_block(q, k, v, u2, valid2, car, in_a):
    z = lax.dot_general(q, _pair_rows(k, in_a), (((1,), (1,)), ((), ())),
                        preferred_element_type=F32)
    sp = _softplus(z)
    l1m = -sp if valid2 is None else jnp.where(valid2, -sp, 0.0)
    hi, lo = _split_hi_lo(l1m)
    r_a = jnp.dot(jnp.concatenate([hi[:, :LANES], lo[:, :LANES]], axis=1), u2,
                  preferred_element_type=F32)
    r_b = jnp.dot(jnp.concatenate([hi[:, LANES:], lo[:, LANES:]], axis=1), u2,
                  preferred_element_type=F32)
    after = jnp.concatenate([r_a[:, :LANES], r_b[:, :LANES]], axis=1)
    total = jnp.concatenate([r_a[:, LANES:], r_b[:, LANES:]], axis=1)
    w = jnp.exp((z - sp) + after + car)
    if valid2 is not None:
        w = jnp.where(valid2, w, 0.0)
    pv = jnp.dot(w.astype(BF16), _pair_rows(v, in_a), preferred_element_type=F32)
    return pv, total


SB_Q_CHUNK = 1024


def _sb_attn_kernel(q_ref, k_ref, v_ref, u2_ref, o_ref, acc_ref, car_ref):
    t = LANES
    n_pairs = q_ref.shape[2] // LANES
    nq = q_ref.shape[1] // t
    chunk = pl.program_id(1)
    lane = lax.broadcasted_iota(jnp.int32, (t, LANES), 1)
    row = lax.broadcasted_iota(jnp.int32, (t, LANES), 0)
    in_a = lane < SB_HEAD_DIM
    col_minus_row = lane - row

    def q_body(qq, carry):
        qi = chunk * nq + qq
        qs = pl.ds(pl.multiple_of(qq * t, t), t)
        acc_ref[...] = jnp.zeros_like(acc_ref)
        car_ref[...] = jnp.zeros_like(car_ref)

        def cond(st):
            kj, mx = st
            return jnp.logical_and(kj >= 0, mx > EXP_ZERO_ARG)

        def body(st):
            kj, _ = st
            ks = pl.ds(pl.multiple_of(kj * t, t), t)
            valid = col_minus_row < (qi - kj) * t
            valid2 = jnp.concatenate([valid, valid], axis=1)
            mx = None
            for p in range(n_pairs):
                ln = slice(p * LANES, (p + 1) * LANES)
                car = car_ref[p]
                pv, total = _sb_block(q_ref[0, qs, ln], k_ref[0, ks, ln], v_ref[0, ks, ln],
                                      u2_ref[...], valid2, car, in_a)
                acc_ref[p] += pv
                car = car + total
                car_ref[p] = car
                mx = jnp.max(car) if mx is None else jnp.maximum(mx, jnp.max(car))
            return kj - 1, mx

        lax.while_loop(cond, body, (qi, jnp.float32(0.0)))
        o_ref[0, qs, :] = jnp.concatenate([acc_ref[p] for p in range(n_pairs)], axis=1)
        return carry

    lax.fori_loop(0, nq, q_body, 0)


def _sb_attn(q16, k16, v16, u2):
    b, s, w = q16.shape
    qc = min(s, SB_Q_CHUNK)
    n_pairs = w // LANES
    qspec = pl.BlockSpec((1, qc, w), lambda i, c: (i, c, 0))
    kspec = pl.BlockSpec((1, s, w), lambda i, c: (i, 0, 0))
    return pl.pallas_call(
        _sb_attn_kernel,
        grid=(b, s // qc),
        in_specs=[qspec, kspec, kspec, pl.BlockSpec(u2.shape, lambda i, c: (0, 0))],
        out_specs=qspec,
        out_shape=jax.ShapeDtypeStruct((b, s, w), F32),
        scratch_shapes=[pltpu.VMEM((n_pairs, LANES, LANES), F32),
                        pltpu.VMEM((n_pairs, LANES, 2 * LANES), F32)],
        compiler_params=_cparams(("parallel", "arbitrary")),
        name="sb_attn",
    )(q16, k16, v16, u2)


def _lam_value(lq1, lk1, lq2, lk2, lam_init):
    f = lambda a, b: jnp.exp(jnp.sum(a * b, axis=-1, keepdims=True))
    return f(lq1, lk1) - f(lq2, lk2) + lam_init


FIXED_SHIFT_MAX = 60.0
DA_TQ = 256
DA_TK = 512


def _da_attn_kernel(nwin_ref, slope_ref, q_ref, k_ref, v_ref, lq1_ref, lk1_ref, lq2_ref, lk2_ref,
                    o_ref, bias_ref, m_ref, l_ref, acc_ref, *, lam_init):
    tq = min(DA_TQ, q_ref.shape[1])
    tk = min(DA_TK, q_ref.shape[1])
    nq = q_ref.shape[1] // tq
    h = pl.program_id(1)
    slope = slope_ref[h]
    n_max = nwin_ref[h]
    bound = slope_ref[slope_ref.shape[0] - 1]
    bounded = nwin_ref[nwin_ref.shape[0] - 1] == 1
    lane = lax.broadcasted_iota(jnp.int32, (tq, tk), 1)
    row = lax.broadcasted_iota(jnp.int32, (tq, tk), 0)
    col_minus_row = lane - row
    bias_ref[...] = slope * col_minus_row.astype(F32)
    in_a = lax.broadcasted_iota(jnp.int32, (tq, LANES), 1) < DA_HEAD_DIM
    lam = _lam_value(lq1_ref[...], lk1_ref[...], lq2_ref[...], lk2_ref[...], lam_init)
    nt_dims = (((1,), (1,)), ((), ()))

    def q_body(qi, carry):
        q0 = pl.multiple_of(qi * tq, tq)
        end = q0 + tq
        q = q_ref[0, pl.ds(q0, tq), :]
        zero = jnp.zeros_like(q)
        q_maps = (jnp.where(in_a, q, zero), jnp.where(in_a, zero, q))
        m_ref[...] = jnp.full_like(m_ref, NEG_INF)
        l_ref[...] = jnp.zeros_like(l_ref)
        acc_ref[...] = jnp.zeros_like(acc_ref)

        def load(t):
            hi = end - t * tk
            start = pl.multiple_of(jnp.maximum(hi - tk, 0), tq)
            shift = slope * (q0 - start).astype(F32)
            mask = lambda: jnp.logical_and(col_minus_row <= q0 - start, lane < hi - start)
            return k_ref[0, pl.ds(start, tk), :], v_ref[0, pl.ds(start, tk), :], shift, mask

        def scores(i, k, mask):
            s = lax.dot_general(q_maps[i], k, nt_dims, preferred_element_type=F32) + bias_ref[...]
            return s if mask is None else jnp.where(mask(), s, NEG_INF)

        def fixed_shift_windows(ts, masked):
            wins = [load(t) for t in ts]
            for i in range(2):
                ln = slice(i * LANES, (i + 1) * LANES)
                l_add, acc_add = None, None
                for k, v, shift, mask in wins:
                    p = jnp.exp(scores(i, k, mask if masked else None) - (bound + shift))
                    ls = jnp.sum(p, axis=1, keepdims=True)
                    pv = jnp.dot(p.astype(BF16), v, preferred_element_type=F32)
                    l_add = ls if l_add is None else l_add + ls
                    acc_add = pv if acc_add is None else acc_add + pv
                l_ref[:, ln] += l_add
                acc_ref[:, ln] += acc_add

        def running_max_window(t, c):
            k, v, shift, mask = load(t)
            for i in range(2):
                ln = slice(i * LANES, (i + 1) * LANES)
                s = scores(i, k, mask)
                m_prev = m_ref[:, ln]
                m_new = jnp.maximum(m_prev, jnp.max(s, axis=1, keepdims=True) - shift)
                p = jnp.exp(s - (m_new[:, :1] + shift))
                alpha = jnp.exp(m_prev - m_new)
                l_ref[:, ln] = alpha * l_ref[:, ln] + jnp.sum(p, axis=1, keepdims=True)
                m_ref[:, ln] = m_new
                acc_ref[:, ln] = alpha * acc_ref[:, ln] + jnp.dot(p.astype(BF16), v, preferred_element_type=F32)
            return c

        n_all = lax.div(end + (tk - 1), tk)
        n = jnp.minimum(n_all, n_max)

        def fixed_shift_tile():
            fixed_shift_windows([0], True)
            tail = jnp.logical_and(jnp.logical_and(n == n_all, n > 1), end - (n_all - 1) * tk < tk)
            n_plain = n - 1 - tail.astype(jnp.int32)

            def pair(j, c):
                fixed_shift_windows([1 + 2 * j, 2 + 2 * j], False)
                return c

            lax.fori_loop(0, lax.div(n_plain, 2), pair, 0)

            @pl.when(lax.rem(n_plain, 2) == 1)
            def _():
                fixed_shift_windows([n_plain], False)

            @pl.when(tail)
            def _():
                fixed_shift_windows([n - 1], True)

        def running_max_tile():
            lax.fori_loop(0, n, running_max_window, 0)

        lax.cond(bounded, fixed_shift_tile, running_max_tile)
        o = acc_ref[...] / l_ref[...]
        o_ref[0, pl.ds(q0, tq), :] = o[:, :LANES] - lam * o[:, LANES:]
        return carry

    lax.fori_loop(0, nq, q_body, 0)


def _da_attn(q16, k16, v16, nwin, slopes, lams, lam_init):
    b, s, w = q16.shape
    tq, tk = min(DA_TQ, s), min(DA_TK, s)
    spec = pl.BlockSpec((1, s, LANES), lambda i, p, *_: (i, 0, p))
    lspec = pl.BlockSpec((1, DA_HEAD_DIM), lambda i, p, *_: (0, 0))
    return pl.pallas_call(
        functools.partial(_da_attn_kernel, lam_init=lam_init),
        grid_spec=pltpu.PrefetchScalarGridSpec(
            num_scalar_prefetch=1,
            grid=(b, w // LANES),
            in_specs=[pl.BlockSpec(memory_space=pltpu.SMEM), spec, spec, spec,
                      lspec, lspec, lspec, lspec],
            out_specs=spec,
            scratch_shapes=[pltpu.VMEM((tq, tk), F32), pltpu.VMEM((tq, 2 * LANES), F32),
                            pltpu.VMEM((tq, 2 * LANES), F32), pltpu.VMEM((tq, 2 * LANES), F32)],
        ),
        out_shape=jax.ShapeDtypeStruct((b, s, w), F32),
        compiler_params=_cparams(("parallel", "parallel")),
        name="da_attn",
    )(nwin, slopes, q16, k16, v16, *lams)


def _da_window_limits(g_qn, g_kn, slopes, seq):
    tq, tk = min(DA_TQ, seq), min(DA_TK, seq)
    bound = 8.0 * jnp.max(jnp.abs(g_qn)) * jnp.max(jnp.abs(g_kn)) * 1.02 + 0.1
    dist = (2.0 * bound - EXP_ZERO_ARG) / slopes
    nwin = jnp.minimum(jnp.ceil((dist + tq) / tk) + 1.0, 2.0 ** 30).astype(jnp.int32)
    fixed_ok = (2.0 * bound <= FIXED_SHIFT_MAX).astype(jnp.int32)
    return (jnp.concatenate([nwin, fixed_ok[None]]),
            jnp.concatenate([slopes, bound[None].astype(F32)]))


def _merge_kernel(sb_ref, da_ref, x_ref, gsb_ref, gda_ref, bd64_ref, bd128_ref, wo_ref, o_ref):
    sb = _chunk_rms(sb_ref[...], bd64_ref[...], gsb_ref[...])
    da = _chunk_rms(da_ref[...], bd128_ref[...], gda_ref[...])
    cat = jnp.concatenate([sb, da], axis=1).astype(BF16)
    o_ref[...] = x_ref[...] + jnp.dot(cat, wo_ref[...], preferred_element_type=F32)


def _merge_out(sb_o, da_o, x, gsb, gda, bd64, bd128, wo16):
    n, d = x.shape
    tm = min(n, 512)
    row = lambda i: (i, 0)
    fix = lambda i: (0, 0)
    return pl.pallas_call(
        _merge_kernel,
        grid=(n // tm,),
        in_specs=[pl.BlockSpec((tm, 512), row), pl.BlockSpec((tm, 512), row), pl.BlockSpec((tm, d), row),
                  pl.BlockSpec((1, 512), fix), pl.BlockSpec((1, 512), fix),
                  pl.BlockSpec((512, 512), fix), pl.BlockSpec((512, 512), fix),
                  pl.BlockSpec(wo16.shape, fix)],
        out_specs=pl.BlockSpec((tm, d), row),
        out_shape=jax.ShapeDtypeStruct((n, d), F32),
        compiler_params=_cparams(("parallel",)),
        name="merge_out",
    )(sb_o, da_o, x, gsb, gda, bd64, bd128, wo16)


def _mem_kv_kernel(x_ref, g_ref, w_ref, gk_ref, bd_ref, k_ref, v_ref):
    m = _rms(x_ref[...], g_ref[...])
    p = jnp.dot(m.astype(BF16), w_ref[...], preferred_element_type=F32)
    k_ref[...] = _chunk_rms(p[:, :512], bd_ref[...], gk_ref[...])
    v_ref[...] = p[:, 512:]


def _mem_kv(mem, g, wkv16, gk, bd128):
    n, d = mem.shape
    tm = min(n, 256)
    row = lambda i: (i, 0)
    fix = lambda i: (0, 0)
    o = jax.ShapeDtypeStruct((n, 512), F32)
    return pl.pallas_call(
        _mem_kv_kernel,
        grid=(n // tm,),
        in_specs=[pl.BlockSpec((tm, d), row), pl.BlockSpec((1, d), fix), pl.BlockSpec(wkv16.shape, fix),
                  pl.BlockSpec((1, 512), fix), pl.BlockSpec((512, 512), fix)],
        out_specs=[pl.BlockSpec((tm, 512), row)] * 2,
        out_shape=[o, o],
        compiler_params=_cparams(("parallel",)),
        name="mem_kv",
    )(mem, g, wkv16, gk, bd128)


def _mem_attn_kernel(x_ref, mk_ref, mv_ref, g_ref, wq_ref, gq_ref, bd_ref, wo_ref, o_ref):
    tm = x_ref.shape[1]
    x = x_ref[0]
    if tm < 8:
        x = jnp.broadcast_to(x[:1], (8, x.shape[1]))
    h = _rms(x, g_ref[...])
    q = jnp.dot(h.astype(BF16), wq_ref[...], preferred_element_type=F32)
    q = (_chunk_rms(q, bd_ref[...], gq_ref[...]) * (LANES ** -0.5)).astype(BF16)
    mk = mk_ref[0].astype(BF16)
    mv = mv_ref[0].astype(BF16)
    outs = []
    for hd in range(4):
        sl = slice(hd * LANES, (hd + 1) * LANES)
        s = lax.dot_general(q[:, sl], mk[:, sl], (((1,), (1,)), ((), ())), preferred_element_type=F32)
        p = jnp.exp(s - jnp.max(s, axis=1, keepdims=True))
        p = p / jnp.sum(p, axis=1, keepdims=True)
        outs.append(jnp.dot(p.astype(BF16), mv[:, sl], preferred_element_type=F32))
    o = jnp.concatenate(outs, axis=1).astype(BF16)
    o_ref[0] = (x + jnp.dot(o, wo_ref[...], preferred_element_type=F32))[:tm]


def _mem_attn(x, mk, mv, g, wq16, gq, bd128, wo16):
    b, t, d = x.shape
    tm = min(t, 512)
    m = mk.shape[1]
    fix = lambda i, j: (0, 0)
    return pl.pallas_call(
        _mem_attn_kernel,
        grid=(b, t // tm),
        in_specs=[pl.BlockSpec((1, tm, d), lambda i, j: (i, j, 0)),
                  pl.BlockSpec((1, m, 512), lambda i, j: (i, 0, 0)),
                  pl.BlockSpec((1, m, 512), lambda i, j: (i, 0, 0)),
                  pl.BlockSpec((1, d), fix), pl.BlockSpec(wq16.shape, fix), pl.BlockSpec((1, 512), fix),
                  pl.BlockSpec((512, 512), fix), pl.BlockSpec(wo16.shape, fix)],
        out_specs=pl.BlockSpec((1, tm, d), lambda i, j: (i, j, 0)),
        out_shape=jax.ShapeDtypeStruct((b, t, d), F32),
        compiler_params=_cparams(("parallel", "parallel")),
        name="mem_attn",
    )(x, mk, mv, g, wq16, gq, bd128, wo16)


def _first_max(vals, lane_f):
    mx = jnp.max(vals, axis=1, keepdims=True)
    idx = jnp.min(jnp.where(vals == mx, lane_f, float(LANES)), axis=1, keepdims=True)
    return mx, idx


def _router_kernel(x_ref, g_ref, whi_ref, wlo_ref, b_ref, route_ref):
    h = _rms(x_ref[...], g_ref[...])
    hi, lo = _split_hi_lo(h)
    lg = (jnp.dot(hi, whi_ref[...], preferred_element_type=F32)
          + jnp.dot(lo, whi_ref[...], preferred_element_type=F32)
          + jnp.dot(hi, wlo_ref[...], preferred_element_type=F32)) + b_ref[...]
    lane = lax.broadcasted_iota(jnp.int32, lg.shape, 1)
    lane_f = lane.astype(F32)
    is_g = lane < N_GROUPS
    gl = jnp.where(is_g, lg, NEG_INF)
    g_max, g_sel = _first_max(gl, lane_f)
    p_g = 1.0 / jnp.sum(jnp.where(is_g, jnp.exp(gl - g_max), 0.0), axis=1, keepdims=True)
    e_lo = N_GROUPS + g_sel * EXPERTS_PER_GROUP
    is_e = jnp.logical_and(lane_f >= e_lo, lane_f < e_lo + EXPERTS_PER_GROUP)
    el = jnp.where(is_e, lg, NEG_INF)
    t1, i1 = _first_max(el, lane_f)
    t2, i2 = _first_max(jnp.where(lane_f == i1, NEG_INF, el), lane_f)
    e2 = jnp.exp(t2 - t1)
    w1 = p_g / (1.0 + e2)
    route_ref[...] = jnp.where(lane == 0, i1 - N_GROUPS,
                               jnp.where(lane == 1, i2 - N_GROUPS,
                                         jnp.where(lane == 2, w1, jnp.where(lane == 3, w1 * e2, 0.0))))


def _router(x, g, whi, wlo, bias):
    n, d = x.shape
    tm = min(n, 512)
    row = lambda i: (i, 0)
    fix = lambda i: (0, 0)
    return pl.pallas_call(
        _router_kernel,
        grid=(n // tm,),
        in_specs=[pl.BlockSpec((tm, d), row), pl.BlockSpec((1, d), fix),
                  pl.BlockSpec((d, LANES), fix), pl.BlockSpec((d, LANES), fix), pl.BlockSpec((1, LANES), fix)],
        out_specs=pl.BlockSpec((tm, LANES), row),
        out_shape=jax.ShapeDtypeStruct((n, LANES), F32),
        compiler_params=_cparams(("parallel",)),
        name="router",
    )(x, g, whi, wlo, bias)


MOE_TM = 256


def _route_plan(route, tm):
    n = route.shape[0]
    n_tiles = (2 * n) // tm + N_EXPERTS
    e_flat = route[:, :2].astype(jnp.int32).reshape(-1)
    order = jnp.argsort(e_flat, stable=True).astype(jnp.int32)
    counts = jnp.sum(e_flat[:, None] == jnp.arange(N_EXPERTS, dtype=jnp.int32)[None, :], axis=0,
                     dtype=jnp.int32)
    tiles_e = (counts + tm - 1) // tm
    tile_end = jnp.cumsum(tiles_e)
    tile_beg = tile_end - tiles_e
    pair_beg = jnp.cumsum(counts) - counts
    j = jnp.arange(n_tiles, dtype=jnp.int32)
    tile_e = jnp.sum(j[:, None] >= tile_end[None, :], axis=1, dtype=jnp.int32)
    te = jnp.minimum(tile_e, N_EXPERTS - 1)
    off = (j - tile_beg[te])[:, None] * tm + jnp.arange(tm, dtype=jnp.int32)[None, :]
    used = (tile_e < N_EXPERTS)[:, None]
    ok = jnp.logical_and(off < counts[te][:, None], used)
    pair = order[jnp.clip(pair_beg[te][:, None] + off, 0, 2 * n - 1)]
    pad = jnp.logical_and(jnp.logical_not(ok), used)
    spare = 2 * n + jnp.cumsum(pad.reshape(-1).astype(jnp.int32)).reshape(pad.shape) - 1
    src = jnp.where(ok, pair // 2, 0)
    dst = jnp.where(ok, (pair % 2) * n + pair // 2, jnp.where(pad, spare, 0))
    idx = jnp.stack([src, dst], axis=1).astype(jnp.int32)
    return tile_e, tile_end[-1:].astype(jnp.int32), idx


def _moe_sparse_kernel(te_ref, nv_ref, g_ref, x_hbm, idx_hbm, wg_ref, wu_ref, wd_ref, y_hbm,
                       idx_smem, xbuf, ybuf, isem, gsem, ssem):
    i = pl.program_id(0)
    n_used = nv_ref[0]
    slot = i % 2
    tm = xbuf.shape[1]

    def idx_copy(tile, s):
        return pltpu.make_async_copy(idx_hbm.at[tile], idx_smem.at[s], isem)

    def gather_row(s, r):
        tok = idx_smem[s, 0, r]
        return pltpu.make_async_copy(x_hbm.at[pl.ds(tok, 1), :], xbuf.at[s, pl.ds(r, 1), :], gsem.at[s])

    def scatter_row(s, r):
        dst = idx_smem[s, 1, r]
        return pltpu.make_async_copy(ybuf.at[s, pl.ds(r, 1), :], y_hbm.at[pl.ds(dst, 1), :], ssem.at[s])

    def gather_all(s):
        return pltpu.make_async_copy(x_hbm.at[pl.ds(0, tm), :], xbuf.at[s], gsem.at[s])

    def scatter_all(s):
        return pltpu.make_async_copy(ybuf.at[s], y_hbm.at[pl.ds(0, tm), :], ssem.at[s])

    def issue(make_row, s):
        def body(r, c):
            make_row(s, r).start()
            return c
        lax.fori_loop(0, tm, body, 0, unroll=8)

    @pl.when(i == 0)
    def _():
        xbuf[1] = jnp.zeros(xbuf.shape[1:], F32)
        spare0 = y_hbm.shape[0] - N_EXPERTS * tm
        fills = [pltpu.make_async_copy(xbuf.at[1], y_hbm.at[pl.ds(spare0 + e * tm, tm), :], ssem.at[1])
                 for e in range(N_EXPERTS)]
        for c in fills:
            c.start()
        for c in fills:
            c.wait()
        idx_copy(0, 0).start()
        idx_copy(0, 0).wait()
        issue(gather_row, 0)

    @pl.when(i < n_used)
    def _():
        nxt = jnp.minimum(i + 1, n_used - 1)
        idx_copy(nxt, 1 - slot).start()
        gather_all(slot).wait()
        idx_copy(nxt, 1 - slot).wait()
        for r in range(tm):
            gather_row(1 - slot, r).start()

        h = _rms(xbuf[slot], g_ref[...]).astype(BF16)
        g = jnp.dot(h, wg_ref[0], preferred_element_type=F32)
        u = jnp.dot(h, wu_ref[0], preferred_element_type=F32)
        a = (g * jax.nn.sigmoid(g)) * u
        y = jnp.dot(a.astype(BF16), wd_ref[0], preferred_element_type=F32)

        @pl.when(i >= 2)
        def _():
            scatter_all(slot).wait()

        ybuf[slot] = y
        for r in range(tm):
            scatter_row(slot, r).start()

        @pl.when(i == n_used - 1)
        def _():
            gather_all(1 - slot).wait()
            scatter_all(slot).wait()

            @pl.when(i >= 1)
            def _():
                scatter_all(1 - slot).wait()


def _moe_sparse(x, route, g, wg16, wu16, wd16):
    n, d = x.shape
    f = wg16.shape[2]
    tm = min(MOE_TM, n)
    tile_e, n_used, idx = _route_plan(route, tm)
    n_tiles = idx.shape[0]
    wmap = lambda i, te, nv: (jnp.minimum(te[i], N_EXPERTS - 1), 0, 0)
    hbm = pl.BlockSpec(memory_space=pl.ANY)
    return pl.pallas_call(
        _moe_sparse_kernel,
        grid_spec=pltpu.PrefetchScalarGridSpec(
            num_scalar_prefetch=2,
            grid=(n_tiles,),
            in_specs=[pl.BlockSpec((1, d), lambda i, te, nv: (0, 0)), hbm, hbm,
                      pl.BlockSpec((1, d, f), wmap), pl.BlockSpec((1, d, f), wmap),
                      pl.BlockSpec((1, f, d), wmap)],
            out_specs=hbm,
            scratch_shapes=[pltpu.SMEM((2, 2, tm), jnp.int32),
                            pltpu.VMEM((2, tm, d), F32), pltpu.VMEM((2, tm, d), F32),
                            pltpu.SemaphoreType.DMA(()), pltpu.SemaphoreType.DMA((2,)),
                            pltpu.SemaphoreType.DMA((2,))],
        ),
        out_shape=jax.ShapeDtypeStruct((2 * n + N_EXPERTS * tm, d), F32),
        compiler_params=_cparams(("arbitrary",)),
        name="moe_sparse",
    )(tile_e, n_used, g, x, idx, wg16, wu16, wd16)


def _moe_combine_kernel(x_ref, y1_ref, y2_ref, route_ref, o_ref):
    r = route_ref[...]
    o_ref[...] = x_ref[...] + r[:, 2:3] * y1_ref[...] + r[:, 3:4] * y2_ref[...]


def _moe_combine(x, y, route):
    n, d = x.shape
    tm = min(n, 512)
    nb = n // tm
    row = lambda i: (i, 0)
    return pl.pallas_call(
        _moe_combine_kernel,
        grid=(nb,),
        in_specs=[pl.BlockSpec((tm, d), row), pl.BlockSpec((tm, d), row),
                  pl.BlockSpec((tm, d), lambda i: (i + nb, 0)), pl.BlockSpec((tm, LANES), row)],
        out_specs=pl.BlockSpec((tm, d), row),
        out_shape=jax.ShapeDtypeStruct((n, d), F32),
        compiler_params=_cparams(("parallel",)),
        name="moe_combine",
    )(x, y, y, route)


PAGES_PER_STEP = 8


def _head_row_mask(width, lanes_per_row):
    lane = lax.broadcasted_iota(jnp.int32, (8, width), 1)
    row = lax.broadcasted_iota(jnp.int32, (8, width), 0)
    return jnp.logical_and(lane >= row * lanes_per_row, lane < (row + 1) * lanes_per_row)


def _sb_decode_kernel(pt_ref, q_ref, u2_ref, kt_hbm, vt_hbm, o_ref, kbuf, vbuf, sem, acc_ref, car_ref,
                      *, layer, n_pages):
    b = pl.program_id(0)
    width = q_ref.shape[2]
    q = jnp.broadcast_to(q_ref[0].astype(F32), (8, width))
    qh = jnp.where(_head_row_mask(width, SB_HEAD_DIM), q, 0.0).astype(BF16)
    acc_ref[...] = jnp.zeros_like(acc_ref)
    car_ref[...] = jnp.zeros_like(car_ref)

    def page_copies(slot):
        page = pt_ref[b * n_pages + slot]
        return (pltpu.make_async_copy(kt_hbm.at[layer, page], kbuf, sem.at[0]),
                pltpu.make_async_copy(vt_hbm.at[layer, page], vbuf, sem.at[1]))

    def cond(st):
        slot, mx = st
        return jnp.logical_and(slot >= 0, mx > EXP_ZERO_ARG)

    def body(st):
        slot, _ = st
        copies = page_copies(slot)
        for c in copies:
            c.start()
        for c in copies:
            c.wait()
        kt = kbuf[...].reshape(width, LANES).astype(BF16)
        vt = vbuf[...].reshape(width, LANES).astype(BF16)
        z = jnp.dot(qh, kt, preferred_element_type=F32)
        sp = _softplus(z)
        hi, lo = _split_hi_lo(-sp)
        r = jnp.dot(jnp.concatenate([hi, lo], axis=1), u2_ref[...], preferred_element_type=F32)
        car = car_ref[...]
        w = jnp.exp((z - sp) + r[:, :LANES] + car)
        acc_ref[...] += lax.dot_general(w.astype(BF16), vt, (((1,), (1,)), ((), ())),
                                        preferred_element_type=F32)
        car = car + r[:, LANES:]
        car_ref[...] = car
        return slot - 1, jnp.max(car)

    lax.while_loop(cond, body, (jnp.int32(n_pages - 1), jnp.float32(0.0)))
    keep = _head_row_mask(width, SB_HEAD_DIM)
    o_ref[0] = jnp.sum(jnp.where(keep, acc_ref[...], 0.0), axis=0, keepdims=True)


def _page_specs(layer, n_pages, page_shape):
    specs = []
    for i in range(PAGES_PER_STEP):
        def imap(b, j, pt, i=i):
            return (layer, pt[b * n_pages + j * PAGES_PER_STEP + i]) + (0,) * len(page_shape)
        specs.append(pl.BlockSpec((1, 1) + tuple(page_shape), imap))
    return specs


def _sb_decode(q, cache_kt, cache_vt, pt_flat, u2, layer, n_pages):
    b, _, width = q.shape
    page_shape = cache_kt.shape[2:]
    assert page_shape[2] == LANES and page_shape[0] * page_shape[1] == width
    qspec = pl.BlockSpec((1, 1, width), lambda i, pt: (i, 0, 0))
    hbm = pl.BlockSpec(memory_space=pl.ANY)
    return pl.pallas_call(
        functools.partial(_sb_decode_kernel, layer=layer, n_pages=n_pages),
        grid_spec=pltpu.PrefetchScalarGridSpec(
            num_scalar_prefetch=1,
            grid=(b,),
            in_specs=[qspec, pl.BlockSpec(u2.shape, lambda i, pt: (0, 0)), hbm, hbm],
            out_specs=qspec,
            scratch_shapes=[pltpu.VMEM(page_shape, F32), pltpu.VMEM(page_shape, F32),
                            pltpu.SemaphoreType.DMA((2,)),
                            pltpu.VMEM((8, width), F32), pltpu.VMEM((8, LANES), F32)],
        ),
        out_shape=jax.ShapeDtypeStruct((b, 1, width), F32),
        compiler_params=_cparams(("arbitrary",)),
        name="sb_decode",
    )(pt_flat, q, u2, cache_kt, cache_vt)


def _da_rows(vec):
    w = vec.shape[1]
    lane = lax.broadcasted_iota(jnp.int32, (8, w), 1)
    row = lax.broadcasted_iota(jnp.int32, (8, w), 0)
    start = (row % 4) * LANES + (row // 4) * DA_HEAD_DIM
    keep = jnp.logical_and(lane >= start, lane < start + DA_HEAD_DIM)
    return jnp.where(keep, jnp.broadcast_to(vec, (8, w)), 0.0)


def _da_decode_kernel(pt_ref, q_ref, kn_ref, vn_ref, lq1_ref, lk1_ref, lq2_ref, lk2_ref, *refs,
                      lam_init, past_len):
    npg = PAGES_PER_STEP
    k_refs, v_refs = refs[:npg], refs[npg:2 * npg]
    o_ref, m_ref, l_ref, acc_ref = refs[2 * npg:]
    j = pl.program_id(1)
    n_heads = kn_ref.shape[1]
    qh = _da_rows(q_ref[0].astype(F32))
    row = lax.broadcasted_iota(jnp.int32, (8, LANES), 0)
    slope = jnp.exp2(-2.0 * ((row % 4) + 1).astype(F32))
    q8f = sum(qh[:, hd * LANES:(hd + 1) * LANES] for hd in range(n_heads))

    @pl.when(j == 0)
    def _():
        kn = jnp.concatenate([kn_ref[0], kn_ref[0]], axis=0)
        s_self = jnp.sum(q8f * kn, axis=1, keepdims=True)
        m_ref[...] = jnp.broadcast_to(s_self, m_ref.shape)
        l_ref[...] = jnp.ones_like(l_ref)
        acc_ref[...] = jnp.concatenate([vn_ref[0], vn_ref[0]], axis=0)

    q8 = q8f.astype(BF16)
    nt_dims = (((1,), (1,)), ((), ()))
    s = jnp.concatenate(
        [lax.dot_general(q8, k_refs[i][0, 0].astype(BF16), nt_dims, preferred_element_type=F32)
         for i in range(npg)], axis=1)
    col = lax.broadcasted_iota(jnp.int32, s.shape, 1)
    row_s = lax.broadcasted_iota(jnp.int32, s.shape, 0)
    k_pos = j * (npg * LANES) + col // n_heads
    s = s - slope[:, :1] * (past_len - k_pos).astype(F32)
    s = jnp.where(col % n_heads == row_s % n_heads, s, NEG_INF)
    m_prev = m_ref[:, :1]
    m_new = jnp.maximum(m_prev, jnp.max(s, axis=1, keepdims=True))
    p = jnp.exp(s - m_new)
    alpha = jnp.exp(m_prev - m_new)
    l_ref[...] = alpha * l_ref[...] + jnp.sum(p, axis=1, keepdims=True)
    m_ref[...] = jnp.broadcast_to(m_new, m_ref.shape)
    p16 = p.astype(BF16)
    rows_pg = k_refs[0].shape[2]
    pv = None
    for i in range(npg):
        d = jnp.dot(p16[:, i * rows_pg:(i + 1) * rows_pg], v_refs[i][0, 0].astype(BF16),
                    preferred_element_type=F32)
        pv = d if pv is None else pv + d
    acc_ref[...] = alpha * acc_ref[...] + pv

    @pl.when(j == pl.num_programs(1) - 1)
    def _():
        lam = _lam_value(lq1_ref[...], lk1_ref[...], lq2_ref[...], lk2_ref[...], lam_init)
        o = acc_ref[...] / l_ref[...]
        o = o[:n_heads] - lam * o[n_heads:]
        o_ref[0] = jnp.concatenate([o[hd:hd + 1] for hd in range(n_heads)], axis=1)


def _da_decode(q, k_new, v_new, cache_k, cache_v, pt_flat, lams, layer, n_pages, lam_init):
    b, _, width = q.shape
    n_heads = width // LANES
    rows = cache_k.shape[2] // n_heads
    assert cache_k.shape[3] == LANES and rows == LANES
    qspec = pl.BlockSpec((1, 1, width), lambda i, j, pt: (i, 0, 0))
    nspec = pl.BlockSpec((1, n_heads, LANES), lambda i, j, pt: (i, 0, 0))
    lspec = pl.BlockSpec((1, DA_HEAD_DIM), lambda i, j, pt: (0, 0))
    pages = _page_specs(layer, n_pages, cache_k.shape[2:])
    as_heads = lambda t: t.reshape(b, n_heads, LANES)
    return pl.pallas_call(
        functools.partial(_da_decode_kernel, lam_init=lam_init, past_len=n_pages * rows),
        grid_spec=pltpu.PrefetchScalarGridSpec(
            num_scalar_prefetch=1,
            grid=(b, n_pages // PAGES_PER_STEP),
            in_specs=[qspec, nspec, nspec, lspec, lspec, lspec, lspec] + pages + pages,
            out_specs=qspec,
            scratch_shapes=[pltpu.VMEM((8, LANES), F32), pltpu.VMEM((8, LANES), F32),
                            pltpu.VMEM((8, LANES), F32)],
        ),
        out_shape=jax.ShapeDtypeStruct((b, 1, width), F32),
        compiler_params=_cparams(("parallel", "arbitrary")),
        name="da_decode",
    )(pt_flat, q, as_heads(k_new), as_heads(v_new), *lams,
      *([cache_k] * PAGES_PER_STEP), *([cache_v] * PAGES_PER_STEP))


def _tile_gain(g, reps):
    return jnp.tile(g.astype(F32), reps)[None, :]


def kernel(x_prompt, x_sample, mem_prompt, cache_sb_k, cache_sb_v, cache_da_k, cache_da_v, cache_mem_k, cache_mem_v, page_table, g_attn, w_in, g_da_qn, g_da_kn, lam_q1, lam_k1, lam_q2, lam_k2, g_sb_out, g_da_out, w_o, g_mem_x, g_mem_src, w_mq, w_mk, w_mv, w_mo, g_mem_qn, g_mem_kn, g_ffn, w_rg, b_rg, w_re, b_re, w_gate, w_up, w_down):
    depth = w_in.shape[0]
    bp, sp, d = x_prompt.shape
    bs, ss, _ = x_sample.shape
    assert ss == 1, "the sample group decodes one new token per sequence"
    n_pages, page = page_table.shape[1], cache_sb_k.shape[2]
    n_mem = mem_prompt.shape[1]

    bd64 = _block_diag_mean(512, 64)
    bd128 = _block_diag_mean(512, 128)
    u2 = _cumsum_matrix()
    slopes = jnp.exp2(-8.0 * jnp.arange(1, 5, dtype=F32) / 4)
    pt_flat = page_table.reshape(-1).astype(jnp.int32)
    csbk = jnp.transpose(cache_sb_k, (0, 1, 3, 4, 2))
    csbv = jnp.transpose(cache_sb_v, (0, 1, 3, 4, 2))
    da_pool = lambda c: c.reshape(c.shape[0], c.shape[1], c.shape[2] * c.shape[3], c.shape[4])
    cdak, cdav = da_pool(cache_da_k), da_pool(cache_da_v)

    xp = x_prompt.reshape(bp * sp, d)
    xs = x_sample.reshape(bs, d)
    mem = mem_prompt.reshape(bp * n_mem, d)
    outs = [[] for _ in range(10)]

    for l in range(depth):
        lam_init = 0.8 - 0.6 * math.exp(-0.3 * l)
        row = lambda v: v[l].astype(F32)[None, :]
        w_in16 = w_in[l].astype(BF16)
        w_o16 = w_o[l].astype(BF16)
        w_mq16, w_mo16 = w_mq[l].astype(BF16), w_mo[l].astype(BF16)
        w_mkv16 = jnp.concatenate([w_mk[l], w_mv[l]], axis=1).astype(BF16)
        wg16, wu16, wd16 = w_gate[l].astype(BF16), w_up[l].astype(BF16), w_down[l].astype(BF16)
        w_r = jnp.zeros((d, LANES), F32).at[:, :N_GROUPS].set(w_rg[l]).at[:, N_GROUPS:N_GROUPS + N_EXPERTS].set(w_re[l])
        w_r_hi = w_r.astype(BF16)
        w_r_lo = (w_r - w_r_hi.astype(F32)).astype(BF16)
        b_r = jnp.zeros((1, LANES), F32).at[0, :N_GROUPS].set(b_rg[l]).at[0, N_GROUPS:N_GROUPS + N_EXPERTS].set(b_re[l])
        gq, gk = _tile_gain(g_da_qn[l], 8), _tile_gain(g_da_kn[l], 8)
        gsb = _tile_gain(g_sb_out[l], 8)
        gda = _tile_gain(g_da_out[l], 4) * (1.0 - lam_init)
        gmq, gmk = _tile_gain(g_mem_qn[l], 4), _tile_gain(g_mem_kn[l], 4)
        lams = (row(lam_q1), row(lam_k1), row(lam_q2), row(lam_k2))
        nwin, slopes_bound = _da_window_limits(g_da_qn[l], g_da_kn[l], slopes, sp)

        def ffn(x):
            route = _router(x, row(g_ffn), w_r_hi, w_r_lo, b_r)
            y = _moe_sparse(x, route, row(g_ffn), wg16, wu16, wd16)
            return _moe_combine(x, y, route)

        (sbk_s, sbv_s, dak_s, dav_s, sbq16, _, _, daq16, _, _) = _in_proj(xs, row(g_attn), w_in16, gq, gk, bd64)
        as1 = lambda t: t.reshape(bs, 1, 512)
        sb_o = _sb_decode(as1(sbq16), csbk, csbv, pt_flat, u2, l, n_pages)
        da_o = _da_decode(as1(daq16), as1(dak_s), as1(dav_s), cdak, cdav, pt_flat, lams, l, n_pages, lam_init)
        xs = _merge_out(sb_o.reshape(bs, 512), da_o.reshape(bs, 512), xs, gsb, gda, bd64, bd128, w_o16)
        xs = _mem_attn(xs.reshape(bs, 1, d), cache_mem_k[l].reshape(bs, n_mem, 512),
                       cache_mem_v[l].reshape(bs, n_mem, 512),
                       row(g_mem_x), w_mq16, gmq, bd128, w_mo16).reshape(bs, d)
        xs = ffn(xs)
        for dst, val in zip(outs[6:], (sbk_s, sbv_s, dak_s, dav_s)):
            dst.append(val)

        (sbk, sbv, dak, dav, sbq16, sbk16, sbv16, daq16, dak16, dav16) = _in_proj(
            xp, row(g_attn), w_in16, gq, gk, bd64)
        as3 = lambda t: t.reshape(bp, sp, 512)
        sb_o = _sb_attn(as3(sbq16), as3(sbk16), as3(sbv16), u2)
        da_o = _da_attn(as3(daq16), as3(dak16), as3(dav16), nwin, slopes_bound, lams, lam_init)
        xp = _merge_out(sb_o.reshape(-1, 512), da_o.reshape(-1, 512), xp, gsb, gda, bd64, bd128, w_o16)
        mk, mv = _mem_kv(mem, row(g_mem_src), w_mkv16, gmk, bd128)
        xp = _mem_attn(xp.reshape(bp, sp, d), mk.reshape(bp, n_mem, 512), mv.reshape(bp, n_mem, 512),
                       row(g_mem_x), w_mq16, gmq, bd128, w_mo16).reshape(-1, d)
        xp = ffn(xp)
        for dst, val in zip(outs[:6], (sbk, sbv, dak, dav, mk, mv)):
            dst.append(val)

    st =lambda lst, shape: jnp.stack(lst, axis=0).reshape(shape)
    return (xp.reshape(bp, sp, d), xs.reshape(bs, 1, d),
            st(outs[0], (depth, bp, sp, 8, 64)), st(outs[1], (depth, bp, sp, 8, 64)),
            st(outs[2], (depth, bp, sp, 4, 128)), st(outs[3], (depth, bp, sp, 4, 128)),
            st(outs[4], (depth, bp, n_mem, 4, 128)), st(outs[5], (depth, bp, n_mem, 4, 128)),
            st(outs[6], (depth, bs, 1, 8, 64)), st(outs[7], (depth, bs, 1, 8, 64)),
            st(outs[8], (depth, bs, 1, 4, 128)), st(outs[9], (depth, bs, 1, 4, 128)))
```
